```python
import math
import jax, jax.numpy as jnp
from jax import lax
import numpy as np

D_MODEL = 1024
BATCH = 2
SEQ = 8192
DEPTH = 4
DEC_BATCH = 128
DEC_SEQ = 4
PAST_LEN = 8192
PAGE_SIZE = 128

N_MIXERS = 3
H_A = 8
DK_A = D_MODEL // H_A
DV_A = D_MODEL // H_A
CHUNK_A = 64
H_B = 8
Q_LORA = D_MODEL // 4
KV_LORA = D_MODEL // 4
DN_B = 128
DR_B = 64
DV_B = 128
ROPE_THETA = 10000.0
Q_BLOCK = 128
MASK_VALUE = -1e30
N_POOL_GROUPS = 4
POOL_WINDOWS = (2, 4, 8, 16)
POOL_GROUP_DIM = D_MODEL // N_POOL_GROUPS
POOL_BUF = max(POOL_WINDOWS) - 1
N_MEM = 256
H_M = 4
DH_M = D_MODEL // H_M
N_GROUPS = 4
E_PER_GROUP = 4
N_EXPERTS = N_GROUPS * E_PER_GROUP
TOP_K_FINE = 2
D_FF_E = D_MODEL // 4
ALPHA = (2.0 * DEPTH) ** 0.25
BETA = (8.0 * DEPTH) ** -0.25
LN_EPS = 1e-5
RMS_EPS = 1e-6
N_LAYERS_A = (DEPTH + 2) // 3
N_LAYERS_B = (DEPTH + 1) // 3
N_LAYERS_C = DEPTH // 3

kernel_name = 'hybrid_hgrn2_mla_pool_hmoe_decoder_step'


def layer_norm(x, g, b):
    xf = x.astype(jnp.float32)
    mu = jnp.mean(xf, -1, keepdims=True)
    var = jnp.mean(jnp.square(xf - mu), -1, keepdims=True)
    return ((xf - mu) * lax.rsqrt(var + LN_EPS) * g + b).astype(x.dtype)


def rms_norm(x, g):
    xf = x.astype(jnp.float32)
    return (xf * lax.rsqrt(jnp.mean(xf * xf, -1, keepdims=True) + RMS_EPS) * g).astype(x.dtype)


def rope_angles(pos):
    inv = ROPE_THETA ** (-jnp.arange(0, DR_B, 2, dtype=jnp.float32) / DR_B)
    ang = pos.astype(jnp.float32)[:, None] * inv[None, :]
    return jnp.cos(ang), jnp.sin(ang)


def apply_rope(x, cos, sin):
    x1, x2 = jnp.split(x.astype(jnp.float32), 2, axis=-1)
    return jnp.concatenate([x1 * cos - x2 * sin, x2 * cos + x1 * sin], -1).astype(x.dtype)


def gla_chunked(q, k, v, log_f, s0):
    B, T, H, K = q.shape
    V = v.shape[-1]
    c = CHUNK_A if T % CHUNK_A == 0 else T
    n = T // c

    def blocks(t):
        return jnp.moveaxis(t.reshape(B, n, c, H, t.shape[-1]), 1, 0)

    causal = jnp.tril(jnp.ones((c, c), dtype=bool))[None, :, :, None, None]

    def step(s, inp):
        qc, kc, vc, gc = inp
        b = jnp.cumsum(gc, axis=1)
        diff = jnp.where(causal, b[:, :, None] - b[:, None, :], 0.0)
        decay = jnp.where(causal, jnp.exp(diff), 0.0)
        a = jnp.einsum('bthk,bshk,btshk->bths', qc, kc, decay)
        o = (jnp.einsum('bthk,bhkv->bthv', qc * jnp.exp(b), s)
             + jnp.einsum('bths,bshv->bthv', a, vc))
        b_last = b[:, -1]
        s_new = (jnp.exp(b_last)[..., None] * s
                 + jnp.einsum('bshk,bshv->bhkv', kc * jnp.exp(b_last[:, None] - b), vc))
        return s_new, o

    s_fin, o = lax.scan(step, s0, (blocks(q), blocks(k), blocks(v), blocks(log_f)))
    return s_fin, jnp.moveaxis(o, 0, 1).reshape(B, T, H, V)


def hgrn2_mixer(x, s0, w_in, lb, g_norm, w_out):
    B, T, _ = x.shape
    u = x @ w_in
    q_raw, f_raw, i_v, g = jnp.split(u, 4, axis=-1)
    z = f_raw.astype(jnp.float32)
    lbf = lb.astype(jnp.float32)
    sig = jax.nn.sigmoid(z)
    f = lbf + (1.0 - lbf) * sig
    log_f = jnp.log(f)
    k = 1.0 - f
    q = jax.nn.silu(q_raw.astype(jnp.float32)) * DK_A ** -0.5
    heads = lambda t: t.reshape(B, T, H_A, -1)
    s_fin, o = gla_chunked(heads(q), heads(k), heads(i_v.astype(jnp.float32)), heads(log_f),
                           s0.astype(jnp.float32))
    o = rms_norm(o.reshape(B, T, D_MODEL), g_norm) * jax.nn.silu(g.astype(jnp.float32))
    return o.astype(x.dtype) @ w_out, s_fin.astype(x.dtype)


def mla_project(x, pos, w_in, g_q, g_kv, w_uq):
    B, T, _ = x.shape
    u = x @ w_in
    cq, ckv, kr = jnp.split(u, [Q_LORA, Q_LORA + KV_LORA], axis=-1)
    q = (rms_norm(cq, g_q) @ w_uq).reshape(B, T, H_B, DN_B + DR_B)
    cos, sin = rope_angles(pos)
    q_nope = q[..., :DN_B]
    q_rope = apply_rope(q[..., DN_B:], cos[:, None], sin[:, None])
    return q_nope, q_rope, rms_norm(ckv, g_kv), apply_rope(kr, cos, sin)


def mla_prompt(x, w_in, g_q, g_kv, w_uq, w_ukv, w_out):
    B, S, _ = x.shape
    pos = jnp.arange(S, dtype=jnp.int32)
    q_nope, q_rope, ckv, kr = mla_project(x, pos, w_in, g_q, g_kv, w_uq)
    w_h = w_ukv.reshape(KV_LORA, H_B, DN_B + DV_B)
    k_nope = jnp.einsum('bsr,rhd->bshd', ckv, w_h[..., :DN_B])
    v = jnp.einsum('bsr,rhd->bshd', ckv, w_h[..., DN_B:])
    nb = S // Q_BLOCK
    scale = (DN_B + DR_B) ** -0.5

    def blk(t):
        return jnp.moveaxis(t.reshape((B, nb, Q_BLOCK) + t.shape[2:]), 1, 0)

    def attend(args):
        qn, qr, qpos = args
        s = jnp.einsum('bqhd,bkhd->bhqk', qn, k_nope) + jnp.einsum('bqhe,bke->bhqk', qr, kr)
        s = s.astype(jnp.float32) * scale
        s = jnp.where(pos[None, None, None, :] <= qpos[None, None, :, None], s, MASK_VALUE)
        p = jax.nn.softmax(s, axis=-1).astype(v.dtype)
        return jnp.einsum('bhqk,bkhv->bqhv', p, v)

    o = lax.map(attend, (blk(q_nope), blk(q_rope), pos.reshape(nb, Q_BLOCK)))
    o = jnp.moveaxis(o, 0, 1).reshape(B, S, H_B * DV_B)
    return o @ w_out, ckv, kr


def mla_sample(x, ckv_past, kr_past, w_in, g_q, g_kv, w_uq, w_ukv, w_out):
    DB, T, _ = x.shape
    pos = PAST_LEN + jnp.arange(T, dtype=jnp.int32)
    q_nope, q_rope, ckv, kr = mla_project(x, pos, w_in, g_q, g_kv, w_uq)
    ckv_all = jnp.concatenate([ckv_past, ckv], axis=1)
    kr_all = jnp.concatenate([kr_past, kr], axis=1)
    w_h = w_ukv.reshape(KV_LORA, H_B, DN_B + DV_B)
    q_lat = jnp.einsum('bthd,rhd->bthr', q_nope, w_h[..., :DN_B])
    s = jnp.einsum('bthr,bsr->bhts', q_lat, ckv_all) + jnp.einsum('bthe,bse->bhts', q_rope, kr_all)
    s = s.astype(jnp.float32) * (DN_B + DR_B) ** -0.5
    key_pos = jnp.arange(PAST_LEN + T, dtype=jnp.int32)
    s = jnp.where(key_pos[None, None, None, :] <= pos[None, None, :, None], s, MASK_VALUE)
    p = jax.nn.softmax(s, axis=-1).astype(ckv_all.dtype)
    o_lat = jnp.einsum('bhts,bsr->bthr', p, ckv_all)
    o = jnp.einsum('bthr,rhv->bthv', o_lat, w_h[..., DN_B:]).reshape(DB, T, H_B * DV_B)
    return o @ w_out, ckv, kr


def pool_mix(u_ext, n_prefix, pos0, w_grp, ls):
    B, L, D = u_ext.shape
    T = L - n_prefix
    uf = u_ext.astype(jnp.float32)
    cs = jnp.concatenate([jnp.zeros((B, 1, D), jnp.float32), jnp.cumsum(uf, axis=1)], axis=1)
    j = jnp.arange(T)
    end = n_prefix + j + 1
    u_self = uf[:, n_prefix:]
    outs = []
    for g, w in enumerate(POOL_WINDOWS):
        sl = slice(g * POOL_GROUP_DIM, (g + 1) * POOL_GROUP_DIM)
        cnt = jnp.minimum(w, pos0 + j + 1)
        start = end - cnt
        csg = cs[..., sl]
        mean = (csg[:, end] - csg[:, start]) / cnt[None, :, None].astype(jnp.float32)
        outs.append(mean - u_self[..., sl])
    p = jnp.concatenate(outs, -1).reshape(B, T, N_POOL_GROUPS, POOL_GROUP_DIM).astype(u_ext.dtype)
    y = jnp.einsum('btgc,gcd->btgd', p, w_grp).reshape(B, T, D)
    return y * ls


def mem_kv(mem, w_k, w_v):
    B, M, _ = mem.shape
    return (mem @ w_k).reshape(B, M, H_M, DH_M), (mem @ w_v).reshape(B, M, H_M, DH_M)


def mem_attend(x, k, v, w_q, w_o):
    B, T, _ = x.shape
    q = (x @ w_q).reshape(B, T, H_M, DH_M)
    s = jnp.einsum('bthd,bmhd->bhtm', q, k).astype(jnp.float32) * DH_M ** -0.5
    p = jax.nn.softmax(s, axis=-1).astype(v.dtype)
    return jnp.einsum('bhtm,bmhd->bthd', p, v).reshape(B, T, D_MODEL) @ w_o


def hier_moe(x, w_rg, b_rg, w_re, b_re, w1, w3, w2):
    shp = x.shape
    t = x.reshape(-1, D_MODEL)
    n = t.shape[0]
    rows = jnp.arange(n)
    lg = (t @ w_rg + b_rg).astype(jnp.float32)
    g_star = jnp.argmax(lg, axis=-1)
    p_group = jax.nn.softmax(lg, axis=-1)[rows, g_star]
    le = (t @ w_re + b_re).astype(jnp.float32).reshape(n, N_GROUPS, E_PER_GROUP)
    le_g = le[rows, g_star]
    top_v, top_i = lax.top_k(le_g, TOP_K_FINE)
    w_top = jax.nn.softmax(top_v, axis=-1) * p_group[:, None]
    expert_id = g_star[:, None] * E_PER_GROUP + top_i
    gates = jnp.sum(jax.nn.one_hot(expert_id, N_EXPERTS, dtype=jnp.float32) * w_top[..., None], axis=1)
    h = jax.nn.silu(jnp.einsum('nd,edf->nef', t, w1)) * jnp.einsum('nd,edf->nef', t, w3)
    h = h * gates[..., None].astype(h.dtype)
    return jnp.einsum('nef,efd->nd', h, w2).reshape(shp)


def setup_inputs(seed: int = 0) -> dict:
    key = jax.random.key(seed)
    keys = iter(jax.random.split(key, 48))

    def nrm(shape, scale):
        return jax.random.normal(next(keys), shape, jnp.float32) * scale

    def gain(shape):
        return 1.0 + nrm(shape, 0.05)

    d = D_MODEL
    n_pages = PAST_LEN // PAGE_SIZE
    n_used = DEC_BATCH * n_pages
    n_phys = n_used + max(1, n_used // 4)
    page_table = jax.random.permutation(next(keys), n_phys)[:n_used].reshape(DEC_BATCH, n_pages).astype(jnp.int32)
    return {
        'x_prompt': nrm((BATCH, SEQ, d), 1.0),
        'x_sample': nrm((DEC_BATCH, DEC_SEQ, d), 1.0),
        'mem_prompt': nrm((BATCH, N_MEM, d), 1.0),
        'state_hgrn': nrm((N_LAYERS_A, DEC_BATCH, H_A, DK_A, DV_A), 0.5),
        'cache_ckv': nrm((n_phys, N_LAYERS_B, PAGE_SIZE, KV_LORA), 1.0),
        'cache_krope': nrm((n_phys, N_LAYERS_B, PAGE_SIZE, DR_B), 1.0),
        'cache_pool': nrm((N_LAYERS_C, DEC_BATCH, POOL_BUF, d), 1.0),
        'cache_mem_k': nrm((DEPTH, DEC_BATCH, N_MEM, H_M, DH_M), 1.0),
        'cache_mem_v': nrm((DEPTH, DEC_BATCH, N_MEM, H_M, DH_M), 1.0),
        'page_table': page_table,
        'ln_g': gain((DEPTH, 3, d)),
        'ln_b': nrm((DEPTH, 3, d), 0.02),
        'a_w_in': nrm((N_LAYERS_A, d, 4 * d), d ** -0.5),
        'a_lb_logits': nrm((N_LAYERS_A, H_A * DK_A), 0.5),
        'a_g_norm': gain((N_LAYERS_A, d)),
        'a_w_out': nrm((N_LAYERS_A, d, d), BETA * d ** -0.5),
        'b_w_in': nrm((N_LAYERS_B, d, Q_LORA + KV_LORA + DR_B), d ** -0.5),
        'b_q_norm': gain((N_LAYERS_B, Q_LORA)),
        'b_kv_norm': gain((N_LAYERS_B, KV_LORA)),
        'b_w_uq': nrm((N_LAYERS_B, Q_LORA, H_B * (DN_B + DR_B)), Q_LORA ** -0.5),
        'b_w_ukv': nrm((N_LAYERS_B, KV_LORA, H_B * (DN_B + DV_B)), KV_LORA ** -0.5),
        'b_w_out': nrm((N_LAYERS_B, H_B * DV_B, d), BETA * (H_B * DV_B) ** -0.5),
        'c_w_in': nrm((N_LAYERS_C, d, d), d ** -0.5),
        'c_w_grp': nrm((N_LAYERS_C, N_POOL_GROUPS, POOL_GROUP_DIM, POOL_GROUP_DIM), BETA * POOL_GROUP_DIM ** -0.5),
        'c_scale': gain((N_LAYERS_C, d)),
        'm_w_q': nrm((DEPTH, d, d), d ** -0.5),
        'm_w_k': nrm((DEPTH, d, d), d ** -0.5),
        'm_w_v': nrm((DEPTH, d, d), d ** -0.5),
        'm_w_o': nrm((DEPTH, d, d), BETA * d ** -0.5),
        'e_w_rg': nrm((DEPTH, d, N_GROUPS), d ** -0.5),
        'e_b_rg': nrm((DEPTH, N_GROUPS), 0.01),
        'e_w_re': nrm((DEPTH, d, N_EXPERTS), d ** -0.5),
        'e_b_re': nrm((DEPTH, N_EXPERTS), 0.01),
        'e_w1': nrm((DEPTH, N_EXPERTS, d, D_FF_E), d ** -0.5),
        'e_w3': nrm((DEPTH, N_EXPERTS, d, D_FF_E), d ** -0.5),
        'e_w2': nrm((DEPTH, N_EXPERTS, D_FF_E, d), BETA * D_FF_E ** -0.5),
    }


def reference(x_prompt, x_sample, mem_prompt, state_hgrn, cache_ckv, cache_krope, cache_pool,
              cache_mem_k, cache_mem_v, page_table, ln_g, ln_b,
              a_w_in, a_lb_logits, a_g_norm, a_w_out,
              b_w_in, b_q_norm, b_kv_norm, b_w_uq, b_w_ukv, b_w_out,
              c_w_in, c_w_grp, c_scale,
              m_w_q, m_w_k, m_w_v, m_w_o,
              e_w_rg, e_b_rg, e_w_re, e_b_re, e_w1, e_w3, e_w2):
    xp, xs = x_prompt, x_sample
    B = xp.shape[0]
    DB = xs.shape[0]
    lb_sm = jax.nn.softmax(a_lb_logits.astype(jnp.float32), axis=0)
    lb_all = jnp.cumsum(lb_sm, axis=0) - lb_sm[0:1]

    hgrn_p, hgrn_s = [], []
    ckv_p, kr_p, ckv_s, kr_s = [], [], [], []
    pool_p, pool_s = [], []
    memk_p, memv_p = [], []

    for i in range(DEPTH):
        kind, li = i % N_MIXERS, i // N_MIXERS
        if kind == 0:
            s0 = jnp.zeros((B, H_A, DK_A, DV_A), xp.dtype)
            yp, sp = hgrn2_mixer(xp, s0, a_w_in[li], lb_all[li], a_g_norm[li], a_w_out[li])
            ys, ss = hgrn2_mixer(xs, state_hgrn[li], a_w_in[li], lb_all[li], a_g_norm[li], a_w_out[li])
            hgrn_p.append(sp)
            hgrn_s.append(ss)
        elif kind == 1:
            yp, ckp, krp = mla_prompt(xp, b_w_in[li], b_q_norm[li], b_kv_norm[li], b_w_uq[li],
                                      b_w_ukv[li], b_w_out[li])
            ckv_past = cache_ckv[page_table, li].reshape(DB, PAST_LEN, KV_LORA)
            kr_past = cache_krope[page_table, li].reshape(DB, PAST_LEN, DR_B)
            ys, cks, krs = mla_sample(xs, ckv_past, kr_past, b_w_in[li], b_q_norm[li], b_kv_norm[li],
                                      b_w_uq[li], b_w_ukv[li], b_w_out[li])
            ckv_p.append(ckp)
            kr_p.append(krp)
            ckv_s.append(cks)
            kr_s.append(krs)
        else:
            up = xp @ c_w_in[li]
            yp = pool_mix(up, 0, 0, c_w_grp[li], c_scale[li])
            us = jnp.concatenate([cache_pool[li], xs @ c_w_in[li]], axis=1)
            ys = pool_mix(us, POOL_BUF, PAST_LEN, c_w_grp[li], c_scale[li])
            pool_p.append(up[:, -POOL_BUF:])
            pool_s.append(us[:, -POOL_BUF:])
        xp = layer_norm(ALPHA * xp + yp, ln_g[i, 0], ln_b[i, 0])
        xs = layer_norm(ALPHA * xs + ys, ln_g[i, 0], ln_b[i, 0])

        mk, mv = mem_kv(mem_prompt, m_w_k[i], m_w_v[i])
        memk_p.append(mk)
        memv_p.append(mv)
        xp = layer_norm(ALPHA * xp + mem_attend(xp, mk, mv, m_w_q[i], m_w_o[i]), ln_g[i, 1], ln_b[i, 1])
        xs = layer_norm(ALPHA * xs + mem_attend(xs, cache_mem_k[i], cache_mem_v[i], m_w_q[i], m_w_o[i]),
                        ln_g[i, 1], ln_b[i, 1])

        xp = layer_norm(ALPHA * xp + hier_moe(xp, e_w_rg[i], e_b_rg[i], e_w_re[i], e_b_re[i],
                                              e_w1[i], e_w3[i], e_w2[i]), ln_g[i, 2], ln_b[i, 2])
        xs = layer_norm(ALPHA * xs + hier_moe(xs, e_w_rg[i], e_b_rg[i], e_w_re[i], e_b_re[i],
                                              e_w1[i], e_w3[i], e_w2[i]), ln_g[i, 2], ln_b[i, 2])

    new_hgrn_prompt = jnp.stack(hgrn_p, axis=0)
    new_hgrn_sample = jnp.stack(hgrn_s, axis=0)
    new_ckv_prompt = jnp.stack(ckv_p, axis=2)
    new_krope_prompt = jnp.stack(kr_p, axis=2)
    new_ckv_sample = jnp.stack(ckv_s, axis=2)
    new_krope_sample = jnp.stack(kr_s, axis=2)
    new_pool_prompt = jnp.stack(pool_p, axis=0)
    new_pool_sample = jnp.stack(pool_s, axis=0)
    new_mem_k_prompt = jnp.stack(memk_p, axis=0)
    new_mem_v_prompt = jnp.stack(memv_p, axis=0)
    return (xp, xs, new_hgrn_prompt, new_hgrn_sample, new_ckv_prompt, new_krope_prompt,
            new_ckv_sample, new_krope_sample, new_pool_prompt, new_pool_sample,
            new_mem_k_prompt, new_mem_v_prompt)
```

```python
import functools
import math

import jax
import jax.numpy as jnp
import numpy as np
from jax import lax
from jax.experimental import pallas as pl
from jax.experimental.pallas import tpu as pltpu

F32 = jnp.float32
BF16 = jnp.bfloat16

D_MODEL = 1024
DEPTH = 4
N_MIXERS = 3
PAGE_SIZE = 128
H_A = 8
DK_A = D_MODEL // H_A
CHUNK_A = 64
H_B = 8
Q_LORA = 256
KV_LORA = 256
DN_B = 128
DR_B = 64
DV_B = 128
ROPE_THETA = 10000.0
MASK_VALUE = -1e30
POOL_WINDOWS = (2, 4, 8, 16)
POOL_GROUP_DIM = D_MODEL // len(POOL_WINDOWS)
POOL_BUF = max(POOL_WINDOWS) - 1
POOL_HALO = 16
N_MEM = 256
H_M = 4
DH_M = D_MODEL // H_M
N_GROUPS = 4
E_PER_GROUP = 4
N_EXPERTS = N_GROUPS * E_PER_GROUP
D_FF_E = D_MODEL // 4
ALPHA = (2.0 * DEPTH) ** 0.25
LN_EPS = 1e-5
RMS_EPS = 1e-6

LANES = 128
VMEM_LIMIT = 56 * 1024 * 1024


def _cparams(sem):
    return pltpu.CompilerParams(dimension_semantics=sem, vmem_limit_bytes=VMEM_LIMIT)


def _dot(a, b):
    return jnp.dot(a, b, preferred_element_type=F32)


def _dot_nt(a, b):
    return lax.dot_general(a, b, (((1,), (1,)), ((), ())), preferred_element_type=F32)


def _dot_tn(a, b):
    return lax.dot_general(a, b, (((0,), (0,)), ((), ())), preferred_element_type=F32)


def _layer_norm(v, g, b):
    mu = jnp.mean(v, axis=-1, keepdims=True)
    c = v - mu
    var = jnp.mean(c * c, axis=-1, keepdims=True)
    return c * lax.rsqrt(var + LN_EPS) * g + b


def _rms_norm(v, g):
    return v * lax.rsqrt(jnp.mean(v * v, axis=-1, keepdims=True) + RMS_EPS) * g


def _silu(v):
    return v * jax.nn.sigmoid(v)


def _linear_kernel(x_ref, w_ref, o_ref):
    o_ref[...] = _dot(x_ref[...].astype(BF16), w_ref[...]).astype(o_ref.dtype)


def _linear(x, w, *, tm, tn, out_dtype, name):
    m, k = x.shape
    n = w.shape[1]
    return pl.pallas_call(
        _linear_kernel,
        out_shape=jax.ShapeDtypeStruct((m, n), out_dtype),
        grid=(n // tn, m // tm),
        in_specs=[pl.BlockSpec((tm, k), lambda j, i: (i, 0)),
                  pl.BlockSpec((k, tn), lambda j, i: (0, j))],
        out_specs=pl.BlockSpec((tm, tn), lambda j, i: (i, j)),
        compiler_params=_cparams(("arbitrary", "arbitrary")),
        name=name,
    )(x, w)


def _linear_res_ln_kernel(x_ref, w_ref, res_ref, g_ref, b_ref, o_ref):
    y = _dot(x_ref[...].astype(BF16), w_ref[...])
    o_ref[...] = _layer_norm(ALPHA * res_ref[...] + y, g_ref[...], b_ref[...])


def _linear_res_ln(x, w, res, g, b, *, tm, name):
    m, k = x.shape
    d = w.shape[1]
    return pl.pallas_call(
        _linear_res_ln_kernel,
        out_shape=jax.ShapeDtypeStruct((m, d), F32),
        grid=(m // tm,),
        in_specs=[pl.BlockSpec((tm, k), lambda i: (i, 0)),
                  pl.BlockSpec((k, d), lambda i: (0, 0)),
                  pl.BlockSpec((tm, d), lambda i: (i, 0)),
                  pl.BlockSpec((1, d), lambda i: (0, 0)),
                  pl.BlockSpec((1, d), lambda i: (0, 0))],
        out_specs=pl.BlockSpec((tm, d), lambda i: (i, 0)),
        compiler_params=_cparams(("arbitrary",)),
        name=name,
    )(x, w, res, g, b)


def _gla_scan_matrix(c):
    levels = int(math.log2(c))
    mats = []
    for lv in range(levels):
        m = 1 << lv
        e = np.zeros((c, c), np.float32)
        for r in range(c):
            mid = (r // (2 * m)) * (2 * m) + m - 1
            if (r // m) % 2 == 1:
                e[r, mid + 1:r + 1] = 1.0
            else:
                e[r, r + 1:mid + 1] = 1.0
        mats.append(e)
    mats.append(np.tril(np.ones((c, c), np.float32)))
    mats.append(np.triu(np.ones((c, c), np.float32), 1))
    e = np.concatenate(mats, axis=0)
    e2 = np.concatenate([e, e], axis=1)
    rows = -(-e2.shape[0] // 16) * 16
    cols = max(LANES, e2.shape[1])
    out = np.zeros((rows, cols), np.float32)
    out[:e2.shape[0], :e2.shape[1]] = e2
    return out


def _gla_kernel(x_ref, s0_ref, win_ref, lb_ref, gn_ref, wout_ref, lng_ref, lnb_ref, e2_ref,
                y_ref, sout_ref, st_ref, q_s, k_s, v_s, g_s, o_s, *, c, valid, rows):
    levels = int(math.log2(c))
    t_step = pl.program_id(1)

    @pl.when(t_step == 0)
    def _():
        for h in range(H_A):
            st_ref[h] = s0_ref[0, h].T

    x = x_ref[0]
    xb = x.astype(BF16)
    lb = lb_ref[...]
    z = _dot(xb, win_ref[:, D_MODEL:2 * D_MODEL])
    f = lb + (1.0 - lb) * jax.nn.sigmoid(z)
    g = jnp.log(f)
    kk = 1.0 - f
    if valid < c:
        ok = (lax.broadcasted_iota(jnp.int32, (rows, D_MODEL), 0) & (c - 1)) < valid
        g = jnp.where(ok, g, 0.0)
        kk = jnp.where(ok, kk, 0.0)
    g_s[...] = g
    k_s[...] = kk
    q_s[...] = _silu(_dot(xb, win_ref[:, 0:D_MODEL])) * (DK_A ** -0.5)
    v_s[...] = _dot(xb, win_ref[:, 2 * D_MODEL:3 * D_MODEL])

    ri = lax.broadcasted_iota(jnp.int32, (c, c), 0)
    ci = lax.broadcasted_iota(jnp.int32, (c, c), 1)
    eye = ri == ci
    same_blk = [(ri >> (lv + 1)) == (ci >> (lv + 1)) for lv in range(levels)]
    rl = lax.broadcasted_iota(jnp.int32, (c, DK_A), 0)
    second = [((rl >> lv) & 1) == 1 for lv in range(levels)]

    def chunk(ic, carry):
        r0 = pl.multiple_of(ic * c, c)
        gc = g_s[pl.ds(r0, c), :]
        ghi = gc.astype(BF16).astype(F32)
        parts = [ghi, gc - ghi]
        pad = e2_ref.shape[1] - 2 * c
        if pad:
            parts.append(jnp.zeros((pad, D_MODEL), F32))
        pw = jnp.exp(_dot(e2_ref[...], jnp.concatenate(parts, axis=0).astype(BF16)))
        qc = q_s[pl.ds(r0, c), :]
        kc = k_s[pl.ds(r0, c), :]
        vc = v_s[pl.ds(r0, c), :]
        for h in range(H_A):
            sl = slice(h * DK_A, (h + 1) * DK_A)
            qh, kh, vh = qc[:, sl], kc[:, sl], vc[:, sl]
            a = jnp.where(eye, _dot_nt(qh.astype(BF16), kh.astype(BF16)), 0.0)
            for lv in range(levels):
                p = pw[lv * c:(lv + 1) * c, sl]
                qe = jnp.where(second[lv], qh * p, 0.0).astype(BF16)
                ke = jnp.where(second[lv], 0.0, kh * p).astype(BF16)
                a = a + jnp.where(same_blk[lv], _dot_nt(qe, ke), 0.0)
            p_b = pw[levels * c:(levels + 1) * c, sl]
            p_s = pw[(levels + 1) * c:(levels + 2) * c, sl]
            st = st_ref[h]
            o_h = _dot_nt((qh * p_b).astype(BF16), st.astype(BF16)) + _dot(a.astype(BF16), vh.astype(BF16))
            o_s[pl.ds(r0, c), sl] = o_h
            st_ref[h] = st * p_b[c - 1:c, :] + _dot_tn(vh.astype(BF16), (kh * p_s).astype(BF16))
        return carry

    lax.fori_loop(0, rows // c, chunk, 0)

    gate = _dot(xb, win_ref[:, 3 * D_MODEL:4 * D_MODEL])
    zz = (_rms_norm(o_s[...], gn_ref[...]) * _silu(gate)).astype(BF16)
    y = _dot(zz, wout_ref[...])
    y_ref[0] = _layer_norm(ALPHA * x + y, lng_ref[...], lnb_ref[...])

    @pl.when(t_step == pl.num_programs(1) - 1)
    def _():
        for h in range(H_A):
            sout_ref[0, h] = st_ref[h].T


def _hgrn_layer(x, s0, s0_off, w_in, lb, g_norm, w_out, ln_g, ln_b, *, c, valid, rows, name):
    nb, t, d = x.shape
    e2 = jnp.asarray(_gla_scan_matrix(c), BF16)
    vec = lambda: pl.BlockSpec((1, d), lambda b, i: (0, 0))
    kern = functools.partial(_gla_kernel, c=c, valid=valid, rows=rows)
    return pl.pallas_call(
        kern,
        out_shape=(jax.ShapeDtypeStruct((nb, t, d), F32),
                   jax.ShapeDtypeStruct((nb, H_A, DK_A, DK_A), F32)),
        grid=(nb, t // rows),
        in_specs=[pl.BlockSpec((1, rows, d), lambda b, i: (b, i, 0)),
                  pl.BlockSpec((1, H_A, DK_A, DK_A), lambda b, i: (b + s0_off, 0, 0, 0)),
                  pl.BlockSpec((d, 4 * d), lambda b, i: (0, 0)),
                  vec(), vec(),
                  pl.BlockSpec((d, d), lambda b, i: (0, 0)),
                  vec(), vec(),
                  pl.BlockSpec(e2.shape, lambda b, i: (0, 0))],
        out_specs=(pl.BlockSpec((1, rows, d), lambda b, i: (b, i, 0)),
                   pl.BlockSpec((1, H_A, DK_A, DK_A), lambda b, i: (b, 0, 0, 0))),
        scratch_shapes=[pltpu.VMEM((H_A, DK_A, DK_A), F32)] + [pltpu.VMEM((rows, d), F32)] * 5,
        compiler_params=_cparams(("arbitrary", "arbitrary")),
        name=name,
    )(x, s0, w_in, lb, g_norm, w_out, ln_g, ln_b, e2)


def _mla_proj_kernel(*refs, with_kv):
    if with_kv:
        (x_ref, win_ref, gq_ref, gkv_ref, wuq_ref, wukv_ref, cos_ref, sin_ref,
         qn_ref, qr_ref, ckv_ref, kr_ref, krb_ref, kn_ref, v_ref) = refs
    else:
        (x_ref, win_ref, gq_ref, gkv_ref, wuq_ref, cos_ref, sin_ref,
         qn_ref, qr_ref, ckv_ref, kr_ref, krb_ref) = refs
    hd = H_B * DN_B
    xb = x_ref[...].astype(BF16)
    u = _dot(xb, win_ref[...])
    cos = cos_ref[...]
    sin = sin_ref[...]
    kr = u[:, 2 * Q_LORA:2 * Q_LORA + LANES] * cos + u[:, 2 * Q_LORA + LANES:2 * Q_LORA + 2 * LANES] * sin
    kr_ref[...] = kr[:, :DR_B]
    krb_ref[...] = kr.astype(BF16)
    ckv = _rms_norm(u[:, Q_LORA:2 * Q_LORA], gkv_ref[...])
    ckv_ref[...] = ckv
    cq = _rms_norm(u[:, :Q_LORA], gq_ref[...]).astype(BF16)
    q = _dot(cq, wuq_ref[...])
    qn_ref[...] = q[:, :hd].astype(BF16)
    cos_h = jnp.concatenate([cos] * H_B, axis=-1)
    sin_h = jnp.concatenate([sin] * H_B, axis=-1)
    qr_ref[...] = (q[:, hd:2 * hd] * cos_h + q[:, 2 * hd:3 * hd] * sin_h).astype(BF16)
    if with_kv:
        kv = _dot(ckv.astype(BF16), wukv_ref[...])
        kn_ref[...] = kv[:, :hd].astype(BF16)
        v_ref[...] = kv[:, hd:].astype(BF16)


def _mla_proj(x, w_in, g_q, g_kv, w_uq, w_ukv, cos, sin, *, tm, with_kv, name):
    m, d = x.shape
    hd = H_B * DN_B
    row = lambda n: pl.BlockSpec((tm, n), lambda i: (i, 0))
    full = lambda a: pl.BlockSpec(a.shape, lambda i: (0, 0))
    ins = [x, w_in, g_q, g_kv, w_uq] + ([w_ukv] if with_kv else []) + [cos, sin]
    in_specs = [row(d), full(w_in), full(g_q), full(g_kv), full(w_uq)] + ([full(w_ukv)] if with_kv else []) \
        + [row(LANES), row(LANES)]
    out_shape = [jax.ShapeDtypeStruct((m, hd), BF16), jax.ShapeDtypeStruct((m, hd), BF16),
                 jax.ShapeDtypeStruct((m, KV_LORA), F32), jax.ShapeDtypeStruct((m, DR_B), F32),
                 jax.ShapeDtypeStruct((m, LANES), BF16)]
    out_specs = [row(hd), row(hd), row(KV_LORA), row(DR_B), row(LANES)]
    if with_kv:
        out_shape += [jax.ShapeDtypeStruct((m, hd), BF16)] * 2
        out_specs += [row(hd), row(hd)]
    return pl.pallas_call(
        functools.partial(_mla_proj_kernel, with_kv=with_kv),
        out_shape=tuple(out_shape),
        grid=(m // tm,),
        in_specs=in_specs,
        out_specs=tuple(out_specs),
        compiler_params=_cparams(("arbitrary",)),
        name=name,
    )(*ins)


def _flash_kernel(qn_ref, qr_ref, kn_ref, kr_ref, v_ref, o_ref, acc_ref, *, tq, scale):
    i = pl.program_id(2)
    q = jnp.concatenate([qn_ref[...], qr_ref[...]], axis=-1)
    acc_ref[...] = jnp.zeros_like(acc_ref)

    def block(j, carry, masked):
        m, l = carry
        r0 = pl.multiple_of(j * tq, tq)
        k = jnp.concatenate([kn_ref[pl.ds(r0, tq), :], kr_ref[pl.ds(r0, tq), :]], axis=-1)
        s = _dot_nt(q, k) * scale
        if masked:
            qi = lax.broadcasted_iota(jnp.int32, (tq, tq), 0)
            ki = lax.broadcasted_iota(jnp.int32, (tq, tq), 1)
            s = jnp.where(ki <= qi, s, MASK_VALUE)
        m_new = jnp.maximum(m, jnp.max(s, axis=-1, keepdims=True))
        a = jnp.exp(m - m_new)
        p = jnp.exp(s - m_new)
        l = a * l + jnp.sum(p, axis=-1, keepdims=True)
        acc_ref[...] = a * acc_ref[...] + _dot(p.astype(BF16), v_ref[pl.ds(r0, tq), :])
        return m_new, l

    init = (jnp.full((tq, 1), -jnp.inf, F32), jnp.zeros((tq, 1), F32))
    carry = lax.fori_loop(0, i, functools.partial(block, masked=False), init)
    _, l = block(i, carry, True)
    o_ref[...] = (acc_ref[...] / l).astype(o_ref.dtype)


def _flash_attention(qn, qr, kn, krb, v, *, batch, seq, tq, name):
    nq = seq // tq
    scale = (DN_B + DR_B) ** -0.5
    qspec = pl.BlockSpec((tq, DN_B), lambda b, h, i: (b * nq + i, h))
    kspec = pl.BlockSpec((seq, DN_B), lambda b, h, i: (b, h))
    return pl.pallas_call(
        functools.partial(_flash_kernel, tq=tq, scale=scale),
        out_shape=jax.ShapeDtypeStruct((batch * seq, H_B * DV_B), BF16),
        grid=(batch, H_B, nq),
        in_specs=[qspec, qspec, kspec,
                  pl.BlockSpec((seq, LANES), lambda b, h, i: (b, 0)),
                  kspec],
        out_specs=pl.BlockSpec((tq, DV_B), lambda b, h, i: (b * nq + i, h)),
        scratch_shapes=[pltpu.VMEM((tq, DV_B), F32)],
        compiler_params=_cparams(("arbitrary", "arbitrary", "arbitrary")),
        name=name,
    )(qn, qr, kn, krb, v)


def _absorb_q_kernel(q_ref, w_ref, o_ref):
    o_ref[...] = _dot_nt(q_ref[...], w_ref[...]).astype(o_ref.dtype)


def _absorb_q(qn, w_ukv, *, name):
    m = qn.shape[0]
    return pl.pallas_call(
        _absorb_q_kernel,
        out_shape=jax.ShapeDtypeStruct((H_B, m, KV_LORA), BF16),
        grid=(H_B,),
        in_specs=[pl.BlockSpec((m, DN_B), lambda h: (0, h)),
                  pl.BlockSpec((KV_LORA, DN_B), lambda h: (0, h))],
        out_specs=pl.BlockSpec((None, m, KV_LORA), lambda h: (h, 0, 0)),
        compiler_params=_cparams(("arbitrary",)),
        name=name,
    )(qn, w_ukv)


def _absorb_o_kernel(o_ref, w_ref, y_ref):
    y_ref[...] = _dot(o_ref[...].astype(BF16), w_ref[...]).astype(y_ref.dtype)


def _absorb_o(o_lat, w_ukv, *, name):
    m = o_lat.shape[1]
    return pl.pallas_call(
        _absorb_o_kernel,
        out_shape=jax.ShapeDtypeStruct((m, H_B * DV_B), BF16),
        grid=(H_B,),
        in_specs=[pl.BlockSpec((None, m, KV_LORA), lambda h: (h, 0, 0)),
                  pl.BlockSpec((KV_LORA, DV_B), lambda h: (0, H_B + h))],
        out_specs=pl.BlockSpec((m, DV_B), lambda h: (0, h)),
        compiler_params=_cparams(("arbitrary",)),
        name=name,
    )(o_lat, w_ukv)


def _mla_decode_kernel(pt_ref, ql_ref, qr_ref, cn_ref, kn_ref, *rest, npg, t_new, scale):
    del pt_ref
    ckv_refs = rest[:npg]
    kr_refs = rest[npg:2 * npg]
    o_ref = rest[2 * npg]
    m_ref, l_ref, acc_ref = rest[2 * npg + 1:]
    gstep = pl.program_id(1)

    @pl.when(gstep == 0)
    def _():
        m_ref[...] = jnp.full_like(m_ref, -jnp.inf)
        l_ref[...] = jnp.zeros_like(l_ref)
        acc_ref[...] = jnp.zeros_like(acc_ref)

    ql = ql_ref[0]
    qr = qr_ref[0]

    def update(s, vals):
        m_old = m_ref[...]
        m_new = jnp.maximum(m_old, jnp.max(s, axis=-1, keepdims=True))
        a = jnp.exp(m_old - m_new)
        p = jnp.exp(s - m_new)
        l_ref[...] = a * l_ref[...] + jnp.sum(p, axis=-1, keepdims=True)
        acc_ref[...] = a * acc_ref[...] + _dot(p.astype(BF16), vals)
        m_ref[...] = m_new

    ck = jnp.concatenate([r[...].astype(BF16) for r in ckv_refs], axis=0)
    kr = jnp.concatenate([r[...].astype(BF16) for r in kr_refs], axis=0)
    update((_dot_nt(ql, ck) + _dot_nt(qr, kr)) * scale, ck)

    @pl.when(gstep == pl.num_programs(1) - 1)
    def _():
        cn = cn_ref[0]
        kn = kn_ref[0]
        s = (_dot_nt(ql, cn) + _dot_nt(qr, kn)) * scale
        n_rows, n_keys = s.shape
        t_row = lax.broadcasted_iota(jnp.int32, (n_rows, n_keys), 0) >> int(math.log2(H_B))
        key = lax.broadcasted_iota(jnp.int32, (n_rows, n_keys), 1)
        s = jnp.where((key <= t_row) & (key < t_new), s, MASK_VALUE)
        update(s, cn)
        o_ref[0] = (acc_ref[...] / l_ref[...]).astype(o_ref.dtype)


def _mla_decode(page_table, q_lat, q_rope, ckv_new, kr_new, cache_ckv, cache_krope, *, layer, npg, t_new, name):
    nb, n_rows, _ = q_lat.shape
    n_pages = page_table.shape[1]
    pt = page_table.reshape(-1)
    scale = (DN_B + DR_B) ** -0.5

    def page_spec(width, j):
        return pl.BlockSpec((None, None, PAGE_SIZE, width),
                            lambda b, g, pt_ref: (pt_ref[b * n_pages + g * npg + j], layer, 0, 0))

    per_b = lambda shape: pl.BlockSpec((1,) + shape, lambda b, g, pt_ref: (b, 0, 0))
    grid_spec = pltpu.PrefetchScalarGridSpec(
        num_scalar_prefetch=1,
        grid=(nb, n_pages // npg),
        in_specs=[per_b((n_rows, KV_LORA)), per_b((n_rows, DR_B)),
                  per_b(ckv_new.shape[1:]), per_b(kr_new.shape[1:])]
        + [page_spec(KV_LORA, j) for j in range(npg)]
        + [page_spec(DR_B, j) for j in range(npg)],
        out_specs=per_b((n_rows, KV_LORA)),
        scratch_shapes=[pltpu.VMEM((n_rows, 1), F32), pltpu.VMEM((n_rows, 1), F32),
                        pltpu.VMEM((n_rows, KV_LORA), F32)],
    )
    return pl.pallas_call(
        functools.partial(_mla_decode_kernel, npg=npg, t_new=t_new, scale=scale),
        out_shape=jax.ShapeDtypeStruct((nb, n_rows, KV_LORA), F32),
        grid_spec=grid_spec,
        compiler_params=_cparams(("arbitrary", "arbitrary")),
        name=name,
    )(pt, q_lat, q_rope, ckv_new, kr_new, *([cache_ckv] * npg), *([cache_krope] * npg))


def _softmax_pv(s, v):
    m = jnp.max(s, axis=-1, keepdims=True)
    e = jnp.exp(s - m)
    return _dot(e.astype(BF16), v) / jnp.sum(e, axis=-1, keepdims=True)


def _memattn_p_kernel(x_ref, k_ref, v_ref, wq_ref, wo_ref, g_ref, b_ref, o_ref):
    x = x_ref[...]
    q = _dot(x.astype(BF16), wq_ref[...]).astype(BF16)
    outs = []
    for h in range(H_M):
        sl = slice(h * DH_M, (h + 1) * DH_M)
        s = _dot_nt(q[:, sl], k_ref[:, sl].astype(BF16)) * (DH_M ** -0.5)
        outs.append(_softmax_pv(s, v_ref[:, sl].astype(BF16)))
    o = jnp.concatenate(outs, axis=-1).astype(BF16)
    o_ref[...] = _layer_norm(ALPHA * x + _dot(o, wo_ref[...]), g_ref[...], b_ref[...])


def _memattn_prompt(x, mk, mv, w_q, w_o, layer, g, b, *, batch, tm, name):
    m, d = x.shape
    nt = m // batch // tm
    wspec = pl.BlockSpec((None, d, d), lambda bb, i: (layer, 0, 0))
    vec = pl.BlockSpec((1, d), lambda bb, i: (0, 0))
    kvspec = pl.BlockSpec((N_MEM, d), lambda bb, i: (bb, 0))
    return pl.pallas_call(
        _memattn_p_kernel,
        out_shape=jax.ShapeDtypeStruct((m, d), F32),
        grid=(batch, nt),
        in_specs=[pl.BlockSpec((tm, d), lambda bb, i: (bb * nt + i, 0)), kvspec, kvspec, wspec, wspec, vec, vec],
        out_specs=pl.BlockSpec((tm, d), lambda bb, i: (bb * nt + i, 0)),
        compiler_params=_cparams(("arbitrary", "arbitrary")),
        name=name,
    )(x, mk, mv, w_q, w_o, g, b)


def _memattn_s_kernel(x_ref, k_ref, v_ref, wq_ref, wo_ref, g_ref, b_ref, o_ref, *, nb, t_new):
    x = x_ref[...]
    rows = x.shape[0]
    q = _dot(x.astype(BF16), wq_ref[...]).astype(BF16)
    row_b = lax.broadcasted_iota(jnp.int32, (rows, DH_M), 0) >> int(math.log2(t_new))
    outs = []
    for h in range(H_M):
        sl = slice(h * DH_M, (h + 1) * DH_M)
        oh = jnp.zeros((rows, DH_M), F32)
        for bb in range(nb):
            s = _dot_nt(q[:, sl], k_ref[bb, :, sl].astype(BF16)) * (DH_M ** -0.5)
            oh = jnp.where(row_b == bb, _softmax_pv(s, v_ref[bb, :, sl].astype(BF16)), oh)
        outs.append(oh)
    o = jnp.concatenate(outs, axis=-1).astype(BF16)
    o_ref[...] = _layer_norm(ALPHA * x + _dot(o, wo_ref[...]), g_ref[...], b_ref[...])


def _memattn_sample(x, ck, cv, w_q, w_o, layer, g, b, *, nb, t_new, name):
    m, d = x.shape
    n_steps = m // (nb * t_new)
    wspec = pl.BlockSpec((None, d, d), lambda i: (layer, 0, 0))
    vec = pl.BlockSpec((1, d), lambda i: (0, 0))
    kvspec = pl.BlockSpec((nb, N_MEM, d), lambda i: (layer * n_steps + i, 0, 0))
    return pl.pallas_call(
        functools.partial(_memattn_s_kernel, nb=nb, t_new=t_new),
        out_shape=jax.ShapeDtypeStruct((m, d), F32),
        grid=(n_steps,),
        in_specs=[pl.BlockSpec((nb * t_new, d), lambda i: (i, 0)), kvspec, kvspec, wspec, wspec, vec, vec],
        out_specs=pl.BlockSpec((nb * t_new, d), lambda i: (i, 0)),
        compiler_params=_cparams(("arbitrary",)),
        name=name,
    )(x, ck, cv, w_q, w_o, g, b)


def _moe_kernel(x_ref, wr_ref, br_ref, w1_ref, w3_ref, w2_ref, g_ref, b_ref, o_ref, xb_ref, gate_ref, acc_ref):
    gi = pl.program_id(1)
    neg = -jnp.inf

    @pl.when(gi == 0)
    def _():
        x = x_ref[...]
        xb_ref[...] = x.astype(BF16)
        lg = jnp.dot(x, wr_ref[...], preferred_element_type=F32, precision=lax.Precision.HIGHEST) + br_ref[...]
        lane_i = lax.broadcasted_iota(jnp.int32, lg.shape, 1)
        lane = lane_i.astype(F32)
        is_g = lane_i < N_GROUPS
        lgm = jnp.where(is_g, lg, neg)
        mg = jnp.max(lgm, axis=-1, keepdims=True)
        g_star = jnp.min(jnp.where(lgm == mg, lane, float(LANES)), axis=-1, keepdims=True)
        p_group = 1.0 / jnp.sum(jnp.where(is_g, jnp.exp(lg - mg), 0.0), axis=-1, keepdims=True)
        e_id = lane_i - N_GROUPS
        grp = (e_id >> int(math.log2(E_PER_GROUP))).astype(F32)
        in_grp = (e_id >= 0) & (e_id < N_EXPERTS) & (grp == g_star)
        le = jnp.where(in_grp, lg, neg)
        v1 = jnp.max(le, axis=-1, keepdims=True)
        i1 = jnp.min(jnp.where(le == v1, lane, float(LANES)), axis=-1, keepdims=True)
        le2 = jnp.where(lane == i1, neg, le)
        v2 = jnp.max(le2, axis=-1, keepdims=True)
        i2 = jnp.min(jnp.where(le2 == v2, lane, float(LANES)), axis=-1, keepdims=True)
        e2 = jnp.exp(v2 - v1)
        den = 1.0 + e2
        gate_ref[...] = jnp.where(lane == i1, p_group / den, jnp.where(lane == i2, p_group * e2 / den, 0.0))
        acc_ref[...] = jnp.zeros_like(acc_ref)

    xb = xb_ref[...]
    gates = gate_ref[...]
    lane = lax.broadcasted_iota(jnp.int32, gates.shape, 1)
    acc = acc_ref[...]
    for j in range(E_PER_GROUP):
        ge = jnp.sum(jnp.where(lane == N_GROUPS + gi * E_PER_GROUP + j, gates, 0.0), axis=-1, keepdims=True)
        hh = _silu(_dot(xb, w1_ref[j])) * _dot(xb, w3_ref[j]) * ge
        acc = acc + _dot(hh.astype(BF16), w2_ref[j])
    acc_ref[...] = acc

    @pl.when(gi == pl.num_programs(1) - 1)
    def _():
        o_ref[...] = _layer_norm(ALPHA * x_ref[...] + acc_ref[...], g_ref[...], b_ref[...])


def _moe(x, w_r, b_r, w1, w3, w2, layer, g, b, *, tm, name):
    m, d = x.shape
    vec = pl.BlockSpec((1, d), lambda i, e: (0, 0))
    return pl.pallas_call(
        _moe_kernel,
        out_shape=jax.ShapeDtypeStruct((m, d), F32),
        grid=(m // tm, N_GROUPS),
        in_specs=[pl.BlockSpec((tm, d), lambda i, e: (i, 0)),
                  pl.BlockSpec((None, d, LANES), lambda i, e: (layer, 0, 0)),
                  pl.BlockSpec((None, 1, LANES), lambda i, e: (layer, 0, 0)),
                  pl.BlockSpec((None, E_PER_GROUP, d, D_FF_E), lambda i, e: (layer, e, 0, 0)),
                  pl.BlockSpec((None, E_PER_GROUP, d, D_FF_E), lambda i, e: (layer, e, 0, 0)),
                  pl.BlockSpec((None, E_PER_GROUP, D_FF_E, d), lambda i, e: (layer, e, 0, 0)),
                  vec, vec],
        out_specs=pl.BlockSpec((tm, d), lambda i, e: (i, 0)),
        scratch_shapes=[pltpu.VMEM((tm, d), BF16), pltpu.VMEM((tm, LANES), F32), pltpu.VMEM((tm, d), F32)],
        compiler_params=_cparams(("arbitrary", "arbitrary")),
        name=name,
    )(x, w_r, b_r, w1, w3, w2, g, b)


def _pool_kernel(u_ref, halo_ref, x_ref, wg_ref, ls_ref, g_ref, b_ref, o_ref, buf_ref, *, tm, from_zero, use_halo):
    t_step = pl.program_id(1)
    u = u_ref[...]
    buf_ref[POOL_HALO:POOL_HALO + tm, :] = u
    if use_halo:
        @pl.when(t_step > 0)
        def _():
            buf_ref[0:POOL_HALO, :] = halo_ref[...]

        @pl.when(t_step == 0)
        def _():
            buf_ref[0:POOL_HALO, :] = jnp.zeros((POOL_HALO, D_MODEL), F32)
    else:
        buf_ref[0:POOL_HALO, :] = jnp.zeros((POOL_HALO, D_MODEL), F32)
    outs = []
    for gi, w in enumerate(POOL_WINDOWS):
        sl = slice(gi * POOL_GROUP_DIM, (gi + 1) * POOL_GROUP_DIM)
        us = u[:, sl]
        acc = us
        for dd in range(1, w):
            acc = acc + buf_ref[POOL_HALO - dd:POOL_HALO - dd + tm, sl]
        if from_zero:
            pos = t_step * tm + lax.broadcasted_iota(jnp.int32, (tm, POOL_GROUP_DIM), 0)
            cnt = jnp.minimum(w, pos + 1).astype(F32)
        else:
            cnt = float(w)
        p = (acc / cnt - us).astype(BF16)
        outs.append(_dot(p, wg_ref[gi]))
    y = jnp.concatenate(outs, axis=-1) * ls_ref[...]
    o_ref[...] = _layer_norm(ALPHA * x_ref[...] + y, g_ref[...], b_ref[...])


def _pool_layer(u, x, w_grp, ls, g, b, *, batch, tm, from_zero, use_halo, name):
    m, d = u.shape
    nt = m // batch // tm
    hb = tm // POOL_HALO
    vec = pl.BlockSpec((1, d), lambda bb, i: (0, 0))
    row = pl.BlockSpec((tm, d), lambda bb, i: (bb * nt + i, 0))
    return pl.pallas_call(
        functools.partial(_pool_kernel, tm=tm, from_zero=from_zero, use_halo=use_halo),
        out_shape=jax.ShapeDtypeStruct((m, d), F32),
        grid=(batch, nt),
        in_specs=[row,
                  pl.BlockSpec((POOL_HALO, d), lambda bb, i: (jnp.maximum((bb * nt + i) * hb - 1, 0), 0)),
                  row,
                  pl.BlockSpec(w_grp.shape, lambda bb, i: (0, 0, 0)),
                  vec, vec, vec],
        out_specs=row,
        scratch_shapes=[pltpu.VMEM((POOL_HALO + tm, d), F32)],
        compiler_params=_cparams(("arbitrary", "arbitrary")),
        name=name,
    )(u, u, x, w_grp, ls, g, b)


def _rot_cols(w):
    half = DR_B // 2
    return jnp.concatenate([-w[..., half:], w[..., :half]], axis=-1)


def _pad_lanes(w):
    return jnp.pad(w, [(0, 0)] * (w.ndim - 1) + [(0, LANES - w.shape[-1])])


def _rope_tables(pos):
    inv = ROPE_THETA ** (-jnp.arange(0, DR_B, 2, dtype=F32) / DR_B)
    ang = pos.astype(F32)[:, None] * inv[None, :]
    cos, sin = jnp.cos(ang), jnp.sin(ang)
    return (_pad_lanes(jnp.concatenate([cos, cos], axis=-1)), _pad_lanes(jnp.concatenate([sin, sin], axis=-1)))


def _mla_weights(w_in, w_uq, w_ukv):
    kr = w_in[:, Q_LORA + KV_LORA:]
    w_in_x = jnp.concatenate([w_in[:, :Q_LORA + KV_LORA], _pad_lanes(kr), _pad_lanes(_rot_cols(kr))], axis=-1)
    wq = w_uq.reshape(Q_LORA, H_B, DN_B + DR_B)
    wq_r = wq[..., DN_B:]
    w_uq_x = jnp.concatenate([wq[..., :DN_B].reshape(Q_LORA, -1),
                              _pad_lanes(wq_r).reshape(Q_LORA, -1),
                              _pad_lanes(_rot_cols(wq_r)).reshape(Q_LORA, -1)], axis=-1)
    wkv = w_ukv.reshape(KV_LORA, H_B, DN_B + DV_B)
    w_ukv_x = jnp.concatenate([wkv[..., :DN_B].reshape(KV_LORA, -1), wkv[..., DN_B:].reshape(KV_LORA, -1)], axis=-1)
    return w_in_x.astype(BF16), w_uq_x.astype(BF16), w_ukv_x.astype(BF16)


def kernel(x_prompt, x_sample, mem_prompt, state_hgrn, cache_ckv, cache_krope, cache_pool, cache_mem_k,
           cache_mem_v, page_table, ln_g, ln_b, a_w_in, a_lb_logits, a_g_norm, a_w_out, b_w_in, b_q_norm,
           b_kv_norm, b_w_uq, b_w_ukv, b_w_out, c_w_in, c_w_grp, c_scale, m_w_q, m_w_k, m_w_v, m_w_o,
           e_w_rg, e_b_rg, e_w_re, e_b_re, e_w1, e_w3, e_w2):
    bsz, seq, d = x_prompt.shape
    dbs, t_new, _ = x_sample.shape
    mp, ms = bsz * seq, dbs * t_new
    t_pad = 8

    a_w_in_b, a_w_out_b = a_w_in.astype(BF16), a_w_out.astype(BF16)
    b_w_out_b = b_w_out.astype(BF16)
    c_w_in_b, c_w_grp_b = c_w_in.astype(BF16), c_w_grp.astype(BF16)
    m_w_q_b, m_w_k_b, m_w_v_b, m_w_o_b = (w.astype(BF16) for w in (m_w_q, m_w_k, m_w_v, m_w_o))
    e_w1_b, e_w3_b, e_w2_b = e_w1.astype(BF16), e_w3.astype(BF16), e_w2.astype(BF16)
    w_router = _pad_lanes(jnp.concatenate([e_w_rg, e_w_re], axis=-1))
    b_router = _pad_lanes(jnp.concatenate([e_b_rg, e_b_re], axis=-1))[:, None, :]

    lb_sm = jax.nn.softmax(a_lb_logits.astype(F32), axis=0)
    lb_all = jnp.cumsum(lb_sm, axis=0) - lb_sm[0:1]

    cmk = cache_mem_k.reshape(DEPTH * dbs, N_MEM, d)
    cmv = cache_mem_v.reshape(DEPTH * dbs, N_MEM, d)
    n_state_layers = state_hgrn.shape[0]
    state_flat = state_hgrn.reshape((n_state_layers * dbs,) + state_hgrn.shape[2:])
    zero_state = jnp.zeros((bsz,) + state_hgrn.shape[2:], F32)

    xp = x_prompt.reshape(mp, d)
    xs = x_sample.reshape(ms, d)
    vec = lambda a: a.reshape(1, -1)

    hgrn_p, hgrn_s, ckv_p, kr_p, ckv_s, kr_s, pool_p, pool_s, memk_p, memv_p = ([] for _ in range(10))

    for i in range(DEPTH):
        kind, li = i % N_MIXERS, i // N_MIXERS
        g0, b0 = vec(ln_g[i, 0]), vec(ln_b[i, 0])
        if kind == 0:
            common = (a_w_in_b[li], vec(lb_all[li]), vec(a_g_norm[li]), a_w_out_b[li], g0, b0)
            yp, sp = _hgrn_layer(xp.reshape(bsz, seq, d), zero_state, 0, *common,
                                 c=CHUNK_A, valid=CHUNK_A, rows=256, name=f"hgrn_prompt_{i}")
            xs_pad = jnp.pad(xs.reshape(dbs, t_new, d), ((0, 0), (0, t_pad - t_new), (0, 0)))
            ys, ss = _hgrn_layer(xs_pad, state_flat, li * dbs, *common,
                                 c=t_pad, valid=t_new, rows=t_pad, name=f"hgrn_sample_{i}")
            xp = yp.reshape(mp, d)
            xs = ys[:, :t_new].reshape(ms, d)
            hgrn_p.append(sp)
            hgrn_s.append(ss)
        elif kind == 1:
            w_in_x, w_uq_x, w_ukv_x = _mla_weights(b_w_in[li], b_w_uq[li], b_w_ukv[li])
            gq, gkv = vec(b_q_norm[li]), vec(b_kv_norm[li])
            cos_p, sin_p = _rope_tables(jnp.arange(seq, dtype=jnp.int32))
            cos_p, sin_p = jnp.tile(cos_p, (bsz, 1)), jnp.tile(sin_p, (bsz, 1))
            qn, qr, ckv, kr, krb, kn, vv = _mla_proj(xp, w_in_x, gq, gkv, w_uq_x, w_ukv_x, cos_p, sin_p,
                                                     tm=512, with_kv=True, name=f"mla_proj_prompt_{i}")
            o = _flash_attention(qn, qr, kn, krb, vv, batch=bsz, seq=seq, tq=512, name=f"mla_flash_{i}")
            xp = _linear_res_ln(o, b_w_out_b[li], xp, g0, b0, tm=512, name=f"mla_out_prompt_{i}")
            ckv_p.append(ckv.reshape(bsz, seq, KV_LORA))
            kr_p.append(kr.reshape(bsz, seq, DR_B))
            past_len = page_table.shape[1] * PAGE_SIZE
            cos_s, sin_s = _rope_tables(past_len + jnp.arange(t_new, dtype=jnp.int32))
            cos_s, sin_s = jnp.tile(cos_s, (dbs, 1)), jnp.tile(sin_s, (dbs, 1))
            qn_s, qr_s, ckv_n, kr_n, _ = _mla_proj(xs, w_in_x, gq, gkv, w_uq_x, None, cos_s, sin_s,
                                                   tm=ms, with_kv=False, name=f"mla_proj_sample_{i}")
            q_lat = _absorb_q(qn_s, w_ukv_x, name=f"mla_absorb_q_{i}")
            q_lat = q_lat.reshape(H_B, dbs, t_new, KV_LORA).transpose(1, 2, 0, 3).reshape(dbs, t_new * H_B, KV_LORA)
            q_rope = qr_s.reshape(dbs, t_new, H_B, LANES)[..., :DR_B].reshape(dbs, t_new * H_B, DR_B)
            pad_t = ((0, 0), (0, 2 * t_pad - t_new), (0, 0))
            ckv_new = jnp.pad(ckv_n.reshape(dbs, t_new, KV_LORA), pad_t).astype(BF16)
            kr_new = jnp.pad(kr_n.reshape(dbs, t_new, DR_B), pad_t).astype(BF16)
            o_lat = _mla_decode(page_table, q_lat, q_rope, ckv_new, kr_new, cache_ckv, cache_krope,
                                layer=li, npg=8, t_new=t_new, name=f"mla_decode_{i}")
            o_lat = o_lat.reshape(dbs, t_new, H_B, KV_LORA).transpose(2, 0, 1, 3).reshape(H_B, ms, KV_LORA)
            o_s = _absorb_o(o_lat, w_ukv_x, name=f"mla_absorb_o_{i}")
            xs = _linear_res_ln(o_s, b_w_out_b[li], xs, g0, b0, tm=ms, name=f"mla_out_sample_{i}")
            ckv_s.append(ckv_n.reshape(dbs, t_new, KV_LORA))
            kr_s.append(kr_n.reshape(dbs, t_new, DR_B))
        else:
            ls = vec(c_scale[li])
            up = _linear(xp, c_w_in_b[li], tm=1024, tn=d, out_dtype=F32, name=f"pool_in_prompt_{i}")
            xp = _pool_layer(up, xp, c_w_grp_b[li], ls, g0, b0, batch=bsz, tm=512,
                             from_zero=True, use_halo=True, name=f"pool_prompt_{i}")
            pool_p.append(up.reshape(bsz, seq, d)[:, -POOL_BUF:])
            us_new = _linear(xs, c_w_in_b[li], tm=ms, tn=d, out_dtype=F32, name=f"pool_in_sample_{i}")
            us = jnp.concatenate([cache_pool[li], us_new.reshape(dbs, t_new, d)], axis=1)
            seg = 2 * POOL_HALO
            lead = seg - t_new - POOL_BUF - t_new
            us_ext = jnp.pad(us, ((0, 0), (lead, t_new), (0, 0))).reshape(dbs * seg, d)
            xs_ext = jnp.pad(xs.reshape(dbs, t_new, d), ((0, 0), (seg - 2 * t_new, t_new), (0, 0))).reshape(dbs * seg, d)
            ys_ext = _pool_layer(us_ext, xs_ext, c_w_grp_b[li], ls, g0, b0, batch=1, tm=min(512, dbs * seg),
                                 from_zero=False, use_halo=False, name=f"pool_sample_{i}")
            xs = ys_ext.reshape(dbs, seg, d)[:, seg - 2 * t_new:seg - t_new].reshape(ms, d)
            pool_s.append(us[:, -POOL_BUF:])

        g1, b1 = vec(ln_g[i, 1]), vec(ln_b[i, 1])
        mem_flat = mem_prompt.reshape(bsz * N_MEM, d)
        mk = _linear(mem_flat, m_w_k_b[i], tm=bsz * N_MEM, tn=d, out_dtype=F32, name=f"mem_k_{i}")
        mv = _linear(mem_flat, m_w_v_b[i], tm=bsz * N_MEM, tn=d, out_dtype=F32, name=f"mem_v_{i}")
        memk_p.append(mk.reshape(bsz, N_MEM, H_M, DH_M))
        memv_p.append(mv.reshape(bsz, N_MEM, H_M, DH_M))
        xp = _memattn_prompt(xp, mk, mv, m_w_q_b, m_w_o_b, i, g1, b1, batch=bsz, tm=512, name=f"memattn_prompt_{i}")
        xs = _memattn_sample(xs, cmk, cmv, m_w_q_b, m_w_o_b, i, g1, b1, nb=4, t_new=t_new, name=f"memattn_sample_{i}")

        g2, b2 = vec(ln_g[i, 2]), vec(ln_b[i, 2])
        xp = _moe(xp, w_router, b_router, e_w1_b, e_w3_b, e_w2_b, i, g2, b2, tm=1024, name=f"moe_prompt_{i}")
        xs = _moe(xs, w_router, b_router, e_w1_b, e_w3_b, e_w2_b, i, g2, b2, tm=ms, name=f"moe_sample_{i}")

    return (xp.reshape(bsz, seq, d), xs.reshape(dbs, t_new, d),
            jnp.stack(hgrn_p, axis=0), jnp.stack(hgrn_s, axis=0),
            jnp.stack(ckv_p, axis=2), jnp.stack(kr_p, axis=2),
            jnp.stack(ckv_s, axis=2), jnp.stack(kr_s, axis=2),
            jnp.stack(pool_p, axis=0), jnp.stack(pool_s, axis=0),
            jnp.stack(memk_p, axis=0), jnp.stack(memv_p, axis=0))
```

```python
import functools
import math

import jax
import jax.numpy as jnp
import numpy as np
from jax import lax
from jax.experimental import pallas as pl
from jax.experimental.pallas import tpu as pltpu

F32 = jnp.float32
BF16 = jnp.bfloat16

D_MODEL = 1024
DEPTH = 4
N_MIXERS = 3
PAGE_SIZE = 128
H_A = 8
DK_A = D_MODEL // H_A
CHUNK_A = 64
H_B = 8
Q_LORA = 256
KV_LORA = 256
DN_B = 128
DR_B = 64
DV_B = 128
ROPE_THETA = 10000.0
MASK_VALUE = -1e30
SM_SCALE = (DN_B + DR_B) ** -0.5
LOG2_E = math.log2(math.e)
POOL_WINDOWS = (2, 4, 8, 16)
POOL_GROUP_DIM = D_MODEL // len(POOL_WINDOWS)
POOL_BUF = max(POOL_WINDOWS) - 1
POOL_HALO = 16
N_MEM = 256
H_M = 4
DH_M = D_MODEL // H_M
N_GROUPS = 4
E_PER_GROUP = 4
N_EXPERTS = N_GROUPS * E_PER_GROUP
D_FF_E = D_MODEL // 4
ALPHA = (2.0 * DEPTH) ** 0.25
LN_EPS = 1e-5
RMS_EPS = 1e-6

LANES = 128
VMEM_LIMIT = 56 * 1024 * 1024


def _cparams(sem):
    return pltpu.CompilerParams(dimension_semantics=sem, vmem_limit_bytes=VMEM_LIMIT)


def _dot(a, b):
    return jnp.dot(a, b, preferred_element_type=F32)


def _dot_nt(a, b):
    return lax.dot_general(a, b, (((1,), (1,)), ((), ())), preferred_element_type=F32)


def _dot_tn(a, b):
    return lax.dot_general(a, b, (((0,), (0,)), ((), ())), preferred_element_type=F32)


def _layer_norm(v, g, b):
    mu = jnp.mean(v, axis=-1, keepdims=True)
    c = v - mu
    var = jnp.mean(c * c, axis=-1, keepdims=True)
    return c * lax.rsqrt(var + LN_EPS) * g + b


def _rms_norm(v, g):
    return v * lax.rsqrt(jnp.mean(v * v, axis=-1, keepdims=True) + RMS_EPS) * g


def _silu(v):
    return v * jax.nn.sigmoid(v)


def _linear_kernel(x_ref, w_ref, o_ref):
    o_ref[...] = _dot(x_ref[...].astype(BF16), w_ref[...]).astype(o_ref.dtype)


def _linear(x, w, *, tm, tn, out_dtype, name):
    m, k = x.shape
    n = w.shape[1]
    return pl.pallas_call(
        _linear_kernel,
        out_shape=jax.ShapeDtypeStruct((m, n), out_dtype),
        grid=(n // tn, m // tm),
        in_specs=[pl.BlockSpec((tm, k), lambda j, i: (i, 0)),
                  pl.BlockSpec((k, tn), lambda j, i: (0, j))],
        out_specs=pl.BlockSpec((tm, tn), lambda j, i: (i, j)),
        compiler_params=_cparams(("arbitrary", "arbitrary")),
        name=name,
    )(x, w)


def _linear_res_ln_kernel(x_ref, w_ref, res_ref, g_ref, b_ref, o_ref):
    y = _dot(x_ref[...].astype(BF16), w_ref[...])
    o_ref[...] = _layer_norm(ALPHA * res_ref[...] + y, g_ref[...], b_ref[...])


def _linear_res_ln(x, w, res, g, b, *, tm, name):
    m, k = x.shape
    d = w.shape[1]
    return pl.pallas_call(
        _linear_res_ln_kernel,
        out_shape=jax.ShapeDtypeStruct((m, d), F32),
        grid=(m // tm,),
        in_specs=[pl.BlockSpec((tm, k), lambda i: (i, 0)),
                  pl.BlockSpec((k, d), lambda i: (0, 0)),
                  pl.BlockSpec((tm, d), lambda i: (i, 0)),
                  pl.BlockSpec((1, d), lambda i: (0, 0)),
                  pl.BlockSpec((1, d), lambda i: (0, 0))],
        out_specs=pl.BlockSpec((tm, d), lambda i: (i, 0)),
        compiler_params=_cparams(("arbitrary",)),
        name=name,
    )(x, w, res, g, b)


def _gla_scan_matrix(c):
    levels = int(math.log2(c))
    mats = []
    for lv in range(levels):
        m = 1 << lv
        e = np.zeros((c, c), np.float32)
        for r in range(c):
            mid = (r // (2 * m)) * (2 * m) + m - 1
            if (r // m) % 2 == 1:
                e[r, mid + 1:r + 1] = 1.0
            else:
                e[r, r + 1:mid + 1] = 1.0
        mats.append(e)
    mats.append(np.tril(np.ones((c, c), np.float32)))
    mats.append(np.triu(np.ones((c, c), np.float32), 1))
    e = np.concatenate(mats, axis=0)
    e2 = np.concatenate([e, e], axis=1)
    rows = -(-e2.shape[0] // 16) * 16
    cols = max(LANES, e2.shape[1])
    out = np.zeros((rows, cols), np.float32)
    out[:e2.shape[0], :e2.shape[1]] = e2
    return out


def _gla_kernel(x_ref, s0_ref, win_ref, lb_ref, gn_ref, wout_ref, lng_ref, lnb_ref, e2_ref,
                y_ref, sout_ref, st_ref, q_s, k_s, v_s, g_s, o_s, *, c, valid, rows):
    levels = int(math.log2(c))
    t_step = pl.program_id(1)

    @pl.when(t_step == 0)
    def _():
        for h in range(H_A):
            st_ref[h] = s0_ref[0, h].T

    x = x_ref[0]
    xb = x.astype(BF16)
    lb = lb_ref[...]
    z = _dot(xb, win_ref[:, D_MODEL:2 * D_MODEL])
    f = lb + (1.0 - lb) * jax.nn.sigmoid(z)
    g = jnp.log(f)
    kk = 1.0 - f
    if valid < c:
        ok = (lax.broadcasted_iota(jnp.int32, (rows, D_MODEL), 0) & (c - 1)) < valid
        g = jnp.where(ok, g, 0.0)
        kk = jnp.where(ok, kk, 0.0)
    g_s[...] = g
    k_s[...] = kk
    q_s[...] = _silu(_dot(xb, win_ref[:, 0:D_MODEL])) * (DK_A ** -0.5)
    v_s[...] = _dot(xb, win_ref[:, 2 * D_MODEL:3 * D_MODEL])

    ri = lax.broadcasted_iota(jnp.int32, (c, c), 0)
    ci = lax.broadcasted_iota(jnp.int32, (c, c), 1)
    eye = ri == ci
    same_blk = [(ri >> (lv + 1)) == (ci >> (lv + 1)) for lv in range(levels)]
    rl = lax.broadcasted_iota(jnp.int32, (c, DK_A), 0)
    second = [((rl >> lv) & 1) == 1 for lv in range(levels)]

    def chunk(ic, carry):
        r0 = pl.multiple_of(ic * c, c)
        gc = g_s[pl.ds(r0, c), :]
        ghi = gc.astype(BF16).astype(F32)
        parts = [ghi, gc - ghi]
        pad = e2_ref.shape[1] - 2 * c
        if pad:
            parts.append(jnp.zeros((pad, D_MODEL), F32))
        pw = jnp.exp(_dot(e2_ref[...], jnp.concatenate(parts, axis=0).astype(BF16)))
        qc = q_s[pl.ds(r0, c), :]
        kc = k_s[pl.ds(r0, c), :]
        vc = v_s[pl.ds(r0, c), :]
        for h in range(H_A):
            sl = slice(h * DK_A, (h + 1) * DK_A)
            qh, kh, vh = qc[:, sl], kc[:, sl], vc[:, sl]
            a = jnp.where(eye, _dot_nt(qh.astype(BF16), kh.astype(BF16)), 0.0)
            for lv in range(levels):
                p = pw[lv * c:(lv + 1) * c, sl]
                qe = jnp.where(second[lv], qh * p, 0.0).astype(BF16)
                ke = jnp.where(second[lv], 0.0, kh * p).astype(BF16)
                a = a + jnp.where(same_blk[lv], _dot_nt(qe, ke), 0.0)
            p_b = pw[levels * c:(levels + 1) * c, sl]
            p_s = pw[(levels + 1) * c:(levels + 2) * c, sl]
            st = st_ref[h]
            o_h = _dot_nt((qh * p_b).astype(BF16), st.astype(BF16)) + _dot(a.astype(BF16), vh.astype(BF16))
            o_s[pl.ds(r0, c), sl] = o_h
            st_ref[h] = st * p_b[c - 1:c, :] + _dot_tn(vh.astype(BF16), (kh * p_s).astype(BF16))
        return carry

    lax.fori_loop(0, rows // c, chunk, 0)

    gate = _dot(xb, win_ref[:, 3 * D_MODEL:4 * D_MODEL])
    zz = (_rms_norm(o_s[...], gn_ref[...]) * _silu(gate)).astype(BF16)
    y = _dot(zz, wout_ref[...])
    y_ref[0] = _layer_norm(ALPHA * x + y, lng_ref[...], lnb_ref[...])

    @pl.when(t_step == pl.num_programs(1) - 1)
    def _():
        for h in range(H_A):
            sout_ref[0, h] = st_ref[h].T


def _hgrn_layer(x, s0, s0_off, w_in, lb, g_norm, w_out, ln_g, ln_b, *, c, valid, rows, name):
    nb, t, d = x.shape
    e2 = jnp.asarray(_gla_scan_matrix(c), BF16)
    vec = lambda: pl.BlockSpec((1, d), lambda b, i: (0, 0))
    kern = functools.partial(_gla_kernel, c=c, valid=valid, rows=rows)
    return pl.pallas_call(
        kern,
        out_shape=(jax.ShapeDtypeStruct((nb, t, d), F32),
                   jax.ShapeDtypeStruct((nb, H_A, DK_A, DK_A), F32)),
        grid=(nb, t // rows),
        in_specs=[pl.BlockSpec((1, rows, d), lambda b, i: (b, i, 0)),
                  pl.BlockSpec((1, H_A, DK_A, DK_A), lambda b, i: (b + s0_off, 0, 0, 0)),
                  pl.BlockSpec((d, 4 * d), lambda b, i: (0, 0)),
                  vec(), vec(),
                  pl.BlockSpec((d, d), lambda b, i: (0, 0)),
                  vec(), vec(),
                  pl.BlockSpec(e2.shape, lambda b, i: (0, 0))],
        out_specs=(pl.BlockSpec((1, rows, d), lambda b, i: (b, i, 0)),
                   pl.BlockSpec((1, H_A, DK_A, DK_A), lambda b, i: (b, 0, 0, 0))),
        scratch_shapes=[pltpu.VMEM((H_A, DK_A, DK_A), F32)] + [pltpu.VMEM((rows, d), F32)] * 5,
        compiler_params=_cparams(("arbitrary", "arbitrary")),
        name=name,
    )(x, s0, w_in, lb, g_norm, w_out, ln_g, ln_b, e2)


def _mla_proj_kernel(*refs, with_kv):
    if with_kv:
        (x_ref, win_ref, gq_ref, gkv_ref, wuq_ref, wukv_ref, cos_ref, sin_ref,
         qn_ref, qr_ref, ckv_ref, kr_ref, krb_ref, kn_ref, v_ref) = refs
    else:
        (x_ref, win_ref, gq_ref, gkv_ref, wuq_ref, cos_ref, sin_ref,
         qn_ref, qr_ref, ckv_ref, kr_ref, krb_ref) = refs
    hd = H_B * DN_B
    xb = x_ref[...].astype(BF16)
    u = _dot(xb, win_ref[...])
    cos = cos_ref[...]
    sin = sin_ref[...]
    kr = u[:, 2 * Q_LORA:2 * Q_LORA + LANES] * cos + u[:, 2 * Q_LORA + LANES:2 * Q_LORA + 2 * LANES] * sin
    kr_ref[...] = kr[:, :DR_B]
    krb_ref[...] = kr.astype(BF16)
    ckv = _rms_norm(u[:, Q_LORA:2 * Q_LORA], gkv_ref[...])
    ckv_ref[...] = ckv
    cq = _rms_norm(u[:, :Q_LORA], gq_ref[...]).astype(BF16)
    q = _dot(cq, wuq_ref[...])
    cos_h = jnp.concatenate([cos] * H_B, axis=-1)
    sin_h = jnp.concatenate([sin] * H_B, axis=-1)
    qn = q[:, :hd]
    qr = q[:, hd:2 * hd] * cos_h + q[:, 2 * hd:3 * hd] * sin_h
    if with_kv:
        qn = qn * (SM_SCALE * LOG2_E)
        qr = qr * (SM_SCALE * LOG2_E)
        kv = _dot(ckv.astype(BF16), wukv_ref[...])
        kn_ref[...] = kv[:, :hd].astype(BF16)
        v_ref[...] = kv[:, hd:].T.astype(BF16)
    qn_ref[...] = qn.astype(BF16)
    qr_ref[...] = qr.astype(BF16)


def _mla_proj(x, w_in, g_q, g_kv, w_uq, w_ukv, cos, sin, *, tm, with_kv, name, seq=None):
    m, d = x.shape
    hd = H_B * DN_B
    row = lambda n: pl.BlockSpec((tm, n), lambda i: (i, 0))
    full = lambda a: pl.BlockSpec(a.shape, lambda i: (0, 0))
    ins = [x, w_in, g_q, g_kv, w_uq] + ([w_ukv] if with_kv else []) + [cos, sin]
    in_specs = [row(d), full(w_in), full(g_q), full(g_kv), full(w_uq)] + ([full(w_ukv)] if with_kv else []) \
        + [row(LANES), row(LANES)]
    out_shape = [jax.ShapeDtypeStruct((m, hd), BF16), jax.ShapeDtypeStruct((m, hd), BF16),
                 jax.ShapeDtypeStruct((m, KV_LORA), F32), jax.ShapeDtypeStruct((m, DR_B), F32),
                 jax.ShapeDtypeStruct((m, LANES), BF16)]
    out_specs = [row(hd), row(hd), row(KV_LORA), row(DR_B), row(LANES)]
    if with_kv:
        nt = seq // tm
        out_shape += [jax.ShapeDtypeStruct((m, hd), BF16), jax.ShapeDtypeStruct((m // seq, nt, hd, tm), BF16)]
        out_specs += [row(hd), pl.BlockSpec((None, None, hd, tm), lambda i: (i // nt, i % nt, 0, 0))]
    return pl.pallas_call(
        functools.partial(_mla_proj_kernel, with_kv=with_kv),
        out_shape=tuple(out_shape),
        grid=(m // tm,),
        in_specs=in_specs,
        out_specs=tuple(out_specs),
        compiler_params=_cparams(("arbitrary",)),
        name=name,
    )(*ins)


def _flash_kernel(qn_ref, qr_ref, kn_ref, kr_ref, vt_ref, o_ref, acc_ref, *, tq, n_split):
    i = pl.program_id(2)
    q = jnp.concatenate([qn_ref[...], qr_ref[...]], axis=-1)
    acc_ref[...] = jnp.zeros_like(acc_ref)
    tqs = tq // n_split

    def block(j, carry, masked):
        r0 = pl.multiple_of(j * tq, tq)
        k = jnp.concatenate([kn_ref[pl.ds(r0, tq), :], kr_ref[pl.ds(r0, tq), :]], axis=-1)
        vt = vt_ref[j]
        out = []
        for sp in range(n_split):
            m, l = carry[sp]
            cols = slice(sp * tqs, (sp + 1) * tqs)
            s = _dot_nt(k, q[cols, :])
            if masked:
                ki = lax.broadcasted_iota(jnp.int32, (tq, tqs), 0)
                qi = lax.broadcasted_iota(jnp.int32, (tq, tqs), 1) + sp * tqs
                s = jnp.where(ki <= qi, s, MASK_VALUE)
            m_new = jnp.maximum(m, jnp.max(s, axis=0, keepdims=True))
            a = jnp.exp2(m - m_new)
            p = jnp.exp2(s - m_new)
            l = a * l + jnp.sum(p, axis=0, keepdims=True)
            acc_ref[:, cols] = a * acc_ref[:, cols] + _dot(vt, p.astype(BF16))
            out.append((m_new, l))
        return tuple(out)

    init = tuple((jnp.full((1, tqs), -jnp.inf, F32), jnp.zeros((1, tqs), F32)) for _ in range(n_split))
    carry = lax.fori_loop(0, i, functools.partial(block, masked=False), init)
    carry = block(i, carry, True)
    l = jnp.concatenate([c[1] for c in carry], axis=-1)
    o_ref[...] = (acc_ref[...] / l).T.astype(o_ref.dtype)


def _flash_attention(qn, qr, kn, krb, vt, *, batch, seq, tq, name):
    nq = seq // tq
    qspec = pl.BlockSpec((tq, DN_B), lambda b, h, i: (b * nq + i, h))
    return pl.pallas_call(
        functools.partial(_flash_kernel, tq=tq, n_split=1),
        out_shape=jax.ShapeDtypeStruct((batch * seq, H_B * DV_B), BF16),
        grid=(batch, H_B, nq),
        in_specs=[qspec, qspec,
                  pl.BlockSpec((seq, DN_B), lambda b, h, i: (b, h)),
                  pl.BlockSpec((seq, LANES), lambda b, h, i: (b, 0)),
                  pl.BlockSpec((None, nq, DV_B, tq), lambda b, h, i: (b, 0, h, 0))],
        out_specs=pl.BlockSpec((tq, DV_B), lambda b, h, i: (b * nq + i, h)),
        scratch_shapes=[pltpu.VMEM((DV_B, tq), F32)],
        compiler_params=_cparams(("arbitrary", "arbitrary", "arbitrary")),
        name=name,
    )(qn, qr, kn, krb, vt)


def _absorb_q_kernel(q_ref, w_ref, o_ref):
    o_ref[...] = _dot_nt(q_ref[...], w_ref[...]).astype(o_ref.dtype)


def _absorb_q(qn, w_ukv, *, name):
    m = qn.shape[0]
    return pl.pallas_call(
        _absorb_q_kernel,
        out_shape=jax.ShapeDtypeStruct((H_B, m, KV_LORA), BF16),
        grid=(H_B,),
        in_specs=[pl.BlockSpec((m, DN_B), lambda h: (0, h)),
                  pl.BlockSpec((KV_LORA, DN_B), lambda h: (0, h))],
        out_specs=pl.BlockSpec((None, m, KV_LORA), lambda h: (h, 0, 0)),
        compiler_params=_cparams(("arbitrary",)),
        name=name,
    )(qn, w_ukv)


def _absorb_o_kernel(o_ref, w_ref, y_ref):
    y_ref[...] = _dot(o_ref[...].astype(BF16), w_ref[...]).astype(y_ref.dtype)


def _absorb_o(o_lat, w_ukv, *, name):
    m = o_lat.shape[1]
    return pl.pallas_call(
        _absorb_o_kernel,
        out_shape=jax.ShapeDtypeStruct((m, H_B * DV_B), BF16),
        grid=(H_B,),
        in_specs=[pl.BlockSpec((None, m, KV_LORA), lambda h: (h, 0, 0)),
                  pl.BlockSpec((KV_LORA, DV_B), lambda h: (0, H_B + h))],
        out_specs=pl.BlockSpec((m, DV_B), lambda h: (0, h)),
        compiler_params=_cparams(("arbitrary",)),
        name=name,
    )(o_lat, w_ukv)


def _mla_decode_kernel(pt_ref, ql_ref, qr_ref, cn_ref, kn_ref, *rest, npg, t_new, scale):
    del pt_ref
    ckv_refs = rest[:npg]
    kr_refs = rest[npg:2 * npg]
    o_ref = rest[2 * npg]
    m_ref, l_ref, acc_ref = rest[2 * npg + 1:]
    gstep = pl.program_id(1)

    @pl.when(gstep == 0)
    def _():
        m_ref[...] = jnp.full_like(m_ref, -jnp.inf)
        l_ref[...] = jnp.zeros_like(l_ref)
        acc_ref[...] = jnp.zeros_like(acc_ref)

    ql = ql_ref[0]
    qr = qr_ref[0]

    def update(s, vals):
        m_old = m_ref[...]
        m_new = jnp.maximum(m_old, jnp.max(s, axis=-1, keepdims=True))
        a = jnp.exp(m_old - m_new)
        p = jnp.exp(s - m_new)
        l_ref[...] = a * l_ref[...] + jnp.sum(p, axis=-1, keepdims=True)
        acc_ref[...] = a * acc_ref[...] + _dot(p.astype(BF16), vals)
        m_ref[...] = m_new

    ck = jnp.concatenate([r[...].astype(BF16) for r in ckv_refs], axis=0)
    kr_t = jnp.concatenate([r[...].astype(BF16) for r in kr_refs], axis=1)
    update((_dot_nt(ql, ck) + _dot(qr, kr_t)) * scale, ck)

    @pl.when(gstep == pl.num_programs(1) - 1)
    def _():
        cn = cn_ref[0]
        kn = kn_ref[0]
        s = (_dot_nt(ql, cn) + _dot_nt(qr, kn)) * scale
        n_rows, n_keys = s.shape
        t_row = lax.broadcasted_iota(jnp.int32, (n_rows, n_keys), 0) >> int(math.log2(H_B))
        key = lax.broadcasted_iota(jnp.int32, (n_rows, n_keys), 1)
        s = jnp.where((key <= t_row) & (key < t_new), s, MASK_VALUE)
        update(s, cn)
        o_ref[0] = (acc_ref[...] / l_ref[...]).astype(o_ref.dtype)


def _mla_decode(page_table, q_lat, q_rope, ckv_new, kr_new, cache_ckv, cache_krope, *, layer, npg, t_new, name):
    nb, n_rows, _ = q_lat.shape
    n_pages = page_table.shape[1]
    pt = page_table.reshape(-1)
    scale = (DN_B + DR_B) ** -0.5

    def page_spec(shape, j):
        return pl.BlockSpec((None, None) + shape,
                            lambda b, g, pt_ref: (pt_ref[b * n_pages + g * npg + j], layer, 0, 0))

    per_b = lambda shape: pl.BlockSpec((1,) + shape, lambda b, g, pt_ref: (b, 0, 0))
    grid_spec = pltpu.PrefetchScalarGridSpec(
        num_scalar_prefetch=1,
        grid=(nb, n_pages // npg),
        in_specs=[per_b((n_rows, KV_LORA)), per_b((n_rows, DR_B)),
                  per_b(ckv_new.shape[1:]), per_b(kr_new.shape[1:])]
        + [page_spec((PAGE_SIZE, KV_LORA), j) for j in range(npg)]
        + [page_spec((DR_B, PAGE_SIZE), j) for j in range(npg)],
        out_specs=per_b((n_rows, KV_LORA)),
        scratch_shapes=[pltpu.VMEM((n_rows, 1), F32), pltpu.VMEM((n_rows, 1), F32),
                        pltpu.VMEM((n_rows, KV_LORA), F32)],
    )
    return pl.pallas_call(
        functools.partial(_mla_decode_kernel, npg=npg, t_new=t_new, scale=scale),
        out_shape=jax.ShapeDtypeStruct((nb, n_rows, KV_LORA), F32),
        grid_spec=grid_spec,
        compiler_params=_cparams(("arbitrary", "arbitrary")),
        name=name,
    )(pt, q_lat, q_rope, ckv_new, kr_new, *([cache_ckv] * npg), *([cache_krope] * npg))


def _softmax_pv(s, v):
    m = jnp.max(s, axis=-1, keepdims=True)
    e = jnp.exp(s - m)
    return _dot(e.astype(BF16), v) / jnp.sum(e, axis=-1, keepdims=True)


def _memattn_p_kernel(x_ref, k_ref, v_ref, wq_ref, wo_ref, g_ref, b_ref, o_ref):
    x = x_ref[...]
    q = _dot(x.astype(BF16), wq_ref[...]).astype(BF16)
    outs = []
    for h in range(H_M):
        sl = slice(h * DH_M, (h + 1) * DH_M)
        s = _dot_nt(q[:, sl], k_ref[:, sl].astype(BF16)) * (DH_M ** -0.5)
        outs.append(_softmax_pv(s, v_ref[:, sl].astype(BF16)))
    o = jnp.concatenate(outs, axis=-1).astype(BF16)
    o_ref[...] = _layer_norm(ALPHA * x + _dot(o, wo_ref[...]), g_ref[...], b_ref[...])


def _memattn_prompt(x, mk, mv, w_q, w_o, layer, g, b, *, batch, tm, name):
    m, d = x.shape
    nt = m // batch // tm
    wspec = pl.BlockSpec((None, d, d), lambda bb, i: (layer, 0, 0))
    vec = pl.BlockSpec((1, d), lambda bb, i: (0, 0))
    kvspec = pl.BlockSpec((N_MEM, d), lambda bb, i: (bb, 0))
    return pl.pallas_call(
        _memattn_p_kernel,
        out_shape=jax.ShapeDtypeStruct((m, d), F32),
        grid=(batch, nt),
        in_specs=[pl.BlockSpec((tm, d), lambda bb, i: (bb * nt + i, 0)), kvspec, kvspec, wspec, wspec, vec, vec],
        out_specs=pl.BlockSpec((tm, d), lambda bb, i: (bb * nt + i, 0)),
        compiler_params=_cparams(("arbitrary", "arbitrary")),
        name=name,
    )(x, mk, mv, w_q, w_o, g, b)


def _mem_cache_rows(c):
    lead = c.shape[:3]
    return c.reshape(lead + (H_M, DH_M // LANES, LANES)).transpose(0, 1, 2, 4, 3, 5).reshape(-1, LANES)


def _memattn_s_kernel(x_ref, k_ref, v_ref, wq_ref, wo_ref, g_ref, b_ref, o_ref, *, nb, t_new):
    x = x_ref[...]
    rows = x.shape[0]
    q = _dot(x.astype(BF16), wq_ref[...]).astype(BF16)
    row_b = lax.broadcasted_iota(jnp.int32, (rows, DH_M), 0) >> int(math.log2(t_new))
    n_dt = DH_M // LANES
    per_m = H_M * n_dt
    per_b = N_MEM * per_m

    def head(ref, bb, h):
        parts = [ref[pl.ds(bb * per_b + dt * H_M + h, N_MEM, stride=per_m), :] for dt in range(n_dt)]
        return jnp.concatenate(parts, axis=-1).astype(BF16)

    outs = []
    for h in range(H_M):
        sl = slice(h * DH_M, (h + 1) * DH_M)
        oh = jnp.zeros((rows, DH_M), F32)
        for bb in range(nb):
            s = _dot_nt(q[:, sl], head(k_ref, bb, h)) * (DH_M ** -0.5)
            oh = jnp.where(row_b == bb, _softmax_pv(s, head(v_ref, bb, h)), oh)
        outs.append(oh)
    o = jnp.concatenate(outs, axis=-1).astype(BF16)
    o_ref[...] = _layer_norm(ALPHA * x + _dot(o, wo_ref[...]), g_ref[...], b_ref[...])


def _memattn_sample(x, ck, cv, w_q, w_o, layer, g, b, *, nb, t_new, name):
    m, d = x.shape
    n_steps = m // (nb * t_new)
    wspec = pl.BlockSpec((None, d, d), lambda i: (layer, 0, 0))
    vec = pl.BlockSpec((1, d), lambda i: (0, 0))
    kvspec = pl.BlockSpec((nb * N_MEM * d // LANES, LANES), lambda i: (layer * n_steps + i, 0))
    return pl.pallas_call(
        functools.partial(_memattn_s_kernel, nb=nb, t_new=t_new),
        out_shape=jax.ShapeDtypeStruct((m, d), F32),
        grid=(n_steps,),
        in_specs=[pl.BlockSpec((nb * t_new, d), lambda i: (i, 0)), kvspec, kvspec, wspec, wspec, vec, vec],
        out_specs=pl.BlockSpec((nb * t_new, d), lambda i: (i, 0)),
        compiler_params=_cparams(("arbitrary",)),
        name=name,
    )(x, ck, cv, w_q, w_o, g, b)


def _moe_kernel(x_ref, wr_ref, br_ref, w1_ref, w3_ref, w2_ref, g_ref, b_ref, o_ref, xb_ref, gate_ref, acc_ref):
    gi = pl.program_id(1)
    neg = -jnp.inf

    @pl.when(gi == 0)
    def _():
        x = x_ref[...]
        xb_ref[...] = x.astype(BF16)
        lg = jnp.dot(x, wr_ref[...], preferred_element_type=F32, precision=lax.Precision.HIGHEST) + br_ref[...]
        lane_i = lax.broadcasted_iota(jnp.int32, lg.shape, 1)
        lane = lane_i.astype(F32)
        is_g = lane_i < N_GROUPS
        lgm = jnp.where(is_g, lg, neg)
        mg = jnp.max(lgm, axis=-1, keepdims=True)
        g_star = jnp.min(jnp.where(lgm == mg, lane, float(LANES)), axis=-1, keepdims=True)
        p_group = 1.0 / jnp.sum(jnp.where(is_g, jnp.exp(lg - mg), 0.0), axis=-1, keepdims=True)
        e_id = lane_i - N_GROUPS
        grp = (e_id >> int(math.log2(E_PER_GROUP))).astype(F32)
        in_grp = (e_id >= 0) & (e_id < N_EXPERTS) & (grp == g_star)
        le = jnp.where(in_grp, lg, neg)
        v1 = jnp.max(le, axis=-1, keepdims=True)
        i1 = jnp.min(jnp.where(le == v1, lane, float(LANES)), axis=-1, keepdims=True)
        le2 = jnp.where(lane == i1, neg, le)
        v2 = jnp.max(le2, axis=-1, keepdims=True)
        i2 = jnp.min(jnp.where(le2 == v2, lane, float(LANES)), axis=-1, keepdims=True)
        e2 = jnp.exp(v2 - v1)
        den = 1.0 + e2
        gate_ref[...] = jnp.where(lane == i1, p_group / den, jnp.where(lane == i2, p_group * e2 / den, 0.0))
        acc_ref[...] = jnp.zeros_like(acc_ref)

    xb = xb_ref[...]
    gates = gate_ref[...]
    lane = lax.broadcasted_iota(jnp.int32, gates.shape, 1)
    acc = acc_ref[...]
    for j in range(E_PER_GROUP):
        ge = jnp.sum(jnp.where(lane == N_GROUPS + gi * E_PER_GROUP + j, gates, 0.0), axis=-1, keepdims=True)
        hh = _silu(_dot(xb, w1_ref[j])) * _dot(xb, w3_ref[j]) * ge
        acc = acc + _dot(hh.astype(BF16), w2_ref[j])
    acc_ref[...] = acc

    @pl.when(gi == pl.num_programs(1) - 1)
    def _():
        o_ref[...] = _layer_norm(ALPHA * x_ref[...] + acc_ref[...], g_ref[...], b_ref[...])


def _moe(x, w_r, b_r, w1, w3, w2, layer, g, b, *, tm, name):
    m, d = x.shape
    vec = pl.BlockSpec((1, d), lambda i, e: (0, 0))
    return pl.pallas_call(
        _moe_kernel,
        out_shape=jax.ShapeDtypeStruct((m, d), F32),
        grid=(m // tm, N_GROUPS),
        in_specs=[pl.BlockSpec((tm, d), lambda i, e: (i, 0)),
                  pl.BlockSpec((None, d, LANES), lambda i, e: (layer, 0, 0)),
                  pl.BlockSpec((None, 1, LANES), lambda i, e: (layer, 0, 0)),
                  pl.BlockSpec((None, E_PER_GROUP, d, D_FF_E), lambda i, e: (layer, e, 0, 0)),
                  pl.BlockSpec((None, E_PER_GROUP, d, D_FF_E), lambda i, e: (layer, e, 0, 0)),
                  pl.BlockSpec((None, E_PER_GROUP, D_FF_E, d), lambda i, e: (layer, e, 0, 0)),
                  vec, vec],
        out_specs=pl.BlockSpec((tm, d), lambda i, e: (i, 0)),
        scratch_shapes=[pltpu.VMEM((tm, d), BF16), pltpu.VMEM((tm, LANES), F32), pltpu.VMEM((tm, d), F32)],
        compiler_params=_cparams(("arbitrary", "arbitrary")),
        name=name,
    )(x, w_r, b_r, w1, w3, w2, g, b)


def _pool_kernel(u_ref, halo_ref, x_ref, wg_ref, ls_ref, g_ref, b_ref, o_ref, buf_ref, *, tm, from_zero, use_halo):
    t_step = pl.program_id(1)
    u = u_ref[...]
    buf_ref[POOL_HALO:POOL_HALO + tm, :] = u
    if use_halo:
        @pl.when(t_step > 0)
        def _():
            buf_ref[0:POOL_HALO, :] = halo_ref[...]

        @pl.when(t_step == 0)
        def _():
            buf_ref[0:POOL_HALO, :] = jnp.zeros((POOL_HALO, D_MODEL), F32)
    else:
        buf_ref[0:POOL_HALO, :] = jnp.zeros((POOL_HALO, D_MODEL), F32)
    outs = []
    for gi, w in enumerate(POOL_WINDOWS):
        sl = slice(gi * POOL_GROUP_DIM, (gi + 1) * POOL_GROUP_DIM)
        us = u[:, sl]
        acc = us
        for dd in range(1, w):
            acc = acc + buf_ref[POOL_HALO - dd:POOL_HALO - dd + tm, sl]
        if from_zero:
            pos = t_step * tm + lax.broadcasted_iota(jnp.int32, (tm, POOL_GROUP_DIM), 0)
            cnt = jnp.minimum(w, pos + 1).astype(F32)
        else:
            cnt = float(w)
        p = (acc / cnt - us).astype(BF16)
        outs.append(_dot(p, wg_ref[gi]))
    y = jnp.concatenate(outs, axis=-1) * ls_ref[...]
    o_ref[...] = _layer_norm(ALPHA * x_ref[...] + y, g_ref[...], b_ref[...])


def _pool_layer(u, x, w_grp, ls, g, b, *, batch, tm, from_zero, use_halo, name):
    m, d = u.shape
    nt = m // batch // tm
    hb = tm // POOL_HALO
    vec = pl.BlockSpec((1, d), lambda bb, i: (0, 0))
    row = pl.BlockSpec((tm, d), lambda bb, i: (bb * nt + i, 0))
    return pl.pallas_call(
        functools.partial(_pool_kernel, tm=tm, from_zero=from_zero, use_halo=use_halo),
        out_shape=jax.ShapeDtypeStruct((m, d), F32),
        grid=(batch, nt),
        in_specs=[row,
                  pl.BlockSpec((POOL_HALO, d), lambda bb, i: (jnp.maximum((bb * nt + i) * hb - 1, 0), 0)),
                  row,
                  pl.BlockSpec(w_grp.shape, lambda bb, i: (0, 0, 0)),
                  vec, vec, vec],
        out_specs=row,
        scratch_shapes=[pltpu.VMEM((POOL_HALO + tm, d), F32)],
        compiler_params=_cparams(("arbitrary", "arbitrary")),
        name=name,
    )(u, u, x, w_grp, ls, g, b)


def _rot_cols(w):
    half = DR_B // 2
    return jnp.concatenate([-w[..., half:], w[..., :half]], axis=-1)


def _pad_lanes(w):
    return jnp.pad(w, [(0, 0)] * (w.ndim - 1) + [(0, LANES - w.shape[-1])])


def _rope_tables(pos):
    inv = ROPE_THETA ** (-jnp.arange(0, DR_B, 2, dtype=F32) / DR_B)
    ang = pos.astype(F32)[:, None] * inv[None, :]
    cos, sin = jnp.cos(ang), jnp.sin(ang)
    return (_pad_lanes(jnp.concatenate([cos, cos], axis=-1)), _pad_lanes(jnp.concatenate([sin, sin], axis=-1)))


def _mla_weights(w_in, w_uq, w_ukv):
    kr = w_in[:, Q_LORA + KV_LORA:]
    w_in_x = jnp.concatenate([w_in[:, :Q_LORA + KV_LORA], _pad_lanes(kr), _pad_lanes(_rot_cols(kr))], axis=-1)
    wq = w_uq.reshape(Q_LORA, H_B, DN_B + DR_B)
    wq_r = wq[..., DN_B:]
    w_uq_x = jnp.concatenate([wq[..., :DN_B].reshape(Q_LORA, -1),
                              _pad_lanes(wq_r).reshape(Q_LORA, -1),
                              _pad_lanes(_rot_cols(wq_r)).reshape(Q_LORA, -1)], axis=-1)
    wkv = w_ukv.reshape(KV_LORA, H_B, DN_B + DV_B)
    w_ukv_x = jnp.concatenate([wkv[..., :DN_B].reshape(KV_LORA, -1), wkv[..., DN_B:].reshape(KV_LORA, -1)], axis=-1)
    return w_in_x.astype(BF16), w_uq_x.astype(BF16), w_ukv_x.astype(BF16)


def kernel(x_prompt, x_sample, mem_prompt, state_hgrn, cache_ckv, cache_krope, cache_pool, cache_mem_k,
           cache_mem_v, page_table, ln_g, ln_b, a_w_in, a_lb_logits, a_g_norm, a_w_out, b_w_in, b_q_norm,
           b_kv_norm, b_w_uq, b_w_ukv, b_w_out, c_w_in, c_w_grp, c_scale, m_w_q, m_w_k, m_w_v, m_w_o,
           e_w_rg, e_b_rg, e_w_re, e_b_re, e_w1, e_w3, e_w2):
    bsz, seq, d = x_prompt.shape
    dbs, t_new, _ = x_sample.shape
    mp, ms = bsz * seq, dbs * t_new
    t_pad = 8

    a_w_in_b, a_w_out_b = a_w_in.astype(BF16), a_w_out.astype(BF16)
    b_w_out_b = b_w_out.astype(BF16)
    c_w_in_b, c_w_grp_b = c_w_in.astype(BF16), c_w_grp.astype(BF16)
    m_w_q_b, m_w_k_b, m_w_v_b, m_w_o_b = (w.astype(BF16) for w in (m_w_q, m_w_k, m_w_v, m_w_o))
    e_w1_b, e_w3_b, e_w2_b = e_w1.astype(BF16), e_w3.astype(BF16), e_w2.astype(BF16)
    w_router = _pad_lanes(jnp.concatenate([e_w_rg, e_w_re], axis=-1))
    b_router = _pad_lanes(jnp.concatenate([e_b_rg, e_b_re], axis=-1))[:, None, :]

    lb_sm = jax.nn.softmax(a_lb_logits.astype(F32), axis=0)
    lb_all = jnp.cumsum(lb_sm, axis=0) - lb_sm[0:1]

    cmk = _mem_cache_rows(cache_mem_k)
    cmv = _mem_cache_rows(cache_mem_v)
    n_state_layers = state_hgrn.shape[0]
    state_flat = state_hgrn.reshape((n_state_layers * dbs,) + state_hgrn.shape[2:])
    zero_state = jnp.zeros((bsz,) + state_hgrn.shape[2:], F32)

    xp = x_prompt.reshape(mp, d)
    xs = x_sample.reshape(ms, d)
    vec = lambda a: a.reshape(1, -1)

    hgrn_p, hgrn_s, ckv_p, kr_p, ckv_s, kr_s, pool_p, pool_s, memk_p, memv_p = ([] for _ in range(10))

    for i in range(DEPTH):
        kind, li = i % N_MIXERS, i // N_MIXERS
        g0, b0 = vec(ln_g[i, 0]), vec(ln_b[i, 0])
        if kind == 0:
            common = (a_w_in_b[li], vec(lb_all[li]), vec(a_g_norm[li]), a_w_out_b[li], g0, b0)
            yp, sp = _hgrn_layer(xp.reshape(bsz, seq, d), zero_state, 0, *common,
                                 c=CHUNK_A, valid=CHUNK_A, rows=256, name=f"hgrn_prompt_{i}")
            xs_pad = jnp.pad(xs.reshape(dbs, t_new, d), ((0, 0), (0, t_pad - t_new), (0, 0)))
            ys, ss = _hgrn_layer(xs_pad, state_flat, li * dbs, *common,
                                 c=t_pad, valid=t_new, rows=t_pad, name=f"hgrn_sample_{i}")
            xp = yp.reshape(mp, d)
            xs = ys[:, :t_new].reshape(ms, d)
            hgrn_p.append(sp)
            hgrn_s.append(ss)
        elif kind == 1:
            w_in_x, w_uq_x, w_ukv_x = _mla_weights(b_w_in[li], b_w_uq[li], b_w_ukv[li])
            gq, gkv = vec(b_q_norm[li]), vec(b_kv_norm[li])
            cos_p, sin_p = _rope_tables(jnp.arange(seq, dtype=jnp.int32))
            cos_p, sin_p = jnp.tile(cos_p, (bsz, 1)), jnp.tile(sin_p, (bsz, 1))
            tq = min(512, seq)
            qn, qr, ckv, kr, krb, kn, vt = _mla_proj(xp, w_in_x, gq, gkv, w_uq_x, w_ukv_x, cos_p, sin_p, tm=tq,
                                                     with_kv=True, seq=seq, name=f"mla_proj_prompt_{i}")
            o = _flash_attention(qn, qr, kn, krb, vt, batch=bsz, seq=seq, tq=tq, name=f"mla_flash_{i}")
            xp = _linear_res_ln(o, b_w_out_b[li], xp, g0, b0, tm=512, name=f"mla_out_prompt_{i}")
            ckv_p.append(ckv.reshape(bsz, seq, KV_LORA))
            kr_p.append(kr.reshape(bsz, seq, DR_B))
            past_len = page_table.shape[1] * PAGE_SIZE
            cos_s, sin_s = _rope_tables(past_len + jnp.arange(t_new, dtype=jnp.int32))
            cos_s, sin_s = jnp.tile(cos_s, (dbs, 1)), jnp.tile(sin_s, (dbs, 1))
            qn_s, qr_s, ckv_n, kr_n, _ = _mla_proj(xs, w_in_x, gq, gkv, w_uq_x, None, cos_s, sin_s,
                                                   tm=ms, with_kv=False, name=f"mla_proj_sample_{i}")
            q_lat = _absorb_q(qn_s, w_ukv_x, name=f"mla_absorb_q_{i}")
            q_lat = q_lat.reshape(H_B, dbs, t_new, KV_LORA).transpose(1, 2, 0, 3).reshape(dbs, t_new * H_B, KV_LORA)
            q_rope = qr_s.reshape(dbs, t_new, H_B, LANES)[..., :DR_B].reshape(dbs, t_new * H_B, DR_B)
            pad_t = ((0, 0), (0, 2 * t_pad - t_new), (0, 0))
            ckv_new = jnp.pad(ckv_n.reshape(dbs, t_new, KV_LORA), pad_t).astype(BF16)
            kr_new = jnp.pad(kr_n.reshape(dbs, t_new, DR_B), pad_t).astype(BF16)
            krope_t = jnp.swapaxes(cache_krope, 2, 3)
            o_lat = _mla_decode(page_table, q_lat, q_rope, ckv_new, kr_new, cache_ckv, krope_t,
                                layer=li, npg=32, t_new=t_new, name=f"mla_decode_{i}")
            o_lat = o_lat.reshape(dbs, t_new, H_B, KV_LORA).transpose(2, 0, 1, 3).reshape(H_B, ms, KV_LORA)
            o_s = _absorb_o(o_lat, w_ukv_x, name=f"mla_absorb_o_{i}")
            xs = _linear_res_ln(o_s, b_w_out_b[li], xs, g0, b0, tm=ms, name=f"mla_out_sample_{i}")
            ckv_s.append(ckv_n.reshape(dbs, t_new, KV_LORA))
            kr_s.append(kr_n.reshape(dbs, t_new, DR_B))
        else:
            ls = vec(c_scale[li])
            up = _linear(xp, c_w_in_b[li], tm=1024, tn=d, out_dtype=F32, name=f"pool_in_prompt_{i}")
            xp = _pool_layer(up, xp, c_w_grp_b[li], ls, g0, b0, batch=bsz, tm=512,
                             from_zero=True, use_halo=True, name=f"pool_prompt_{i}")
            pool_p.append(up.reshape(bsz, seq, d)[:, -POOL_BUF:])
            us_new = _linear(xs, c_w_in_b[li], tm=ms, tn=d, out_dtype=F32, name=f"pool_in_sample_{i}")
            us = jnp.concatenate([cache_pool[li], us_new.reshape(dbs, t_new, d)], axis=1)
            seg = 2 * POOL_HALO
            lead = seg - t_new - POOL_BUF - t_new
            us_ext = jnp.pad(us, ((0, 0), (lead, t_new), (0, 0))).reshape(dbs * seg, d)
            xs_ext = jnp.pad(xs.reshape(dbs, t_new, d), ((0, 0), (seg - 2 * t_new, t_new), (0, 0))).reshape(dbs * seg, d)
            ys_ext = _pool_layer(us_ext, xs_ext, c_w_grp_b[li], ls, g0, b0, batch=1, tm=min(512, dbs * seg),
                                 from_zero=False, use_halo=False, name=f"pool_sample_{i}")
            xs = ys_ext.reshape(dbs, seg, d)[:, seg - 2 * t_new:seg - t_new].reshape(ms, d)
            pool_s.append(us[:, -POOL_BUF:])

        g1, b1 = vec(ln_g[i, 1]), vec(ln_b[i, 1])
        mem_flat = mem_prompt.reshape(bsz * N_MEM, d)
        mk = _linear(mem_flat, m_w_k_b[i], tm=bsz * N_MEM, tn=d, out_dtype=F32, name=f"mem_k_{i}")
        mv = _linear(mem_flat, m_w_v_b[i], tm=bsz * N_MEM, tn=d, out_dtype=F32, name=f"mem_v_{i}")
        memk_p.append(mk.reshape(bsz, N_MEM, H_M, DH_M))
        memv_p.append(mv.reshape(bsz, N_MEM, H_M, DH_M))
        xp = _memattn_prompt(xp, mk, mv, m_w_q_b, m_w_o_b, i, g1, b1, batch=bsz, tm=512, name=f"memattn_prompt_{i}")
        xs = _memattn_sample(xs, cmk, cmv, m_w_q_b, m_w_o_b, i, g1, b1, nb=4, t_new=t_new, name=f"memattn_sample_{i}")

        g2, b2 = vec(ln_g[i, 2]), vec(ln_b[i, 2])
        xp = _moe(xp, w_router, b_router, e_w1_b, e_w3_b, e_w2_b, i, g2, b2, tm=1024, name=f"moe_prompt_{i}")
        xs = _moe(xs, w_router, b_router, e_w1_b, e_w3_b, e_w2_b, i, g2, b2, tm=ms, name=f"moe_sample_{i}")

    return (xp.reshape(bsz, seq, d), xs.reshape(dbs, t_new, d),
            jnp.stack(hgrn_p, axis=0), jnp.stack(hgrn_s, axis=0),
            jnp.stack(ckv_p, axis=2), jnp.stack(kr_p, axis=2),
            jnp.stack(ckv_s, axis=2), jnp.stack(kr_s, axis=2),
            jnp.stack(pool_p, axis=0), jnp.stack(pool_s, axis=0),
            jnp.stack(memk_p, axis=0), jnp.stack(memv_p, axis=0))
```

```python
import functools
import math

import jax
import jax.numpy as jnp
import numpy as np
from jax import lax
from jax.experimental import pallas as pl
from jax.experimental.pallas import tpu as pltpu

F32 = jnp.float32
BF16 = jnp.bfloat16

D_MODEL = 1024
DEPTH = 4
N_MIXERS = 3
PAGE_SIZE = 128
H_A = 8
DK_A = D_MODEL // H_A
CHUNK_A = 64
H_B = 8
Q_LORA = 256
KV_LORA = 256
DN_B = 128
DR_B = 64
DV_B = 128
ROPE_THETA = 10000.0
MASK_VALUE = -1e30
SM_SCALE = (DN_B + DR_B) ** -0.5
LOG2_E = math.log2(math.e)
POOL_WINDOWS = (2, 4, 8, 16)
POOL_GROUP_DIM = D_MODEL // len(POOL_WINDOWS)
POOL_BUF = max(POOL_WINDOWS) - 1
POOL_HALO = 16
N_MEM = 256
H_M = 4
DH_M = D_MODEL // H_M
N_GROUPS = 4
E_PER_GROUP = 4
N_EXPERTS = N_GROUPS * E_PER_GROUP
D_FF_E = D_MODEL // 4
MOE_BLOCK = 128
ALPHA = (2.0 * DEPTH) ** 0.25
LN_EPS = 1e-5
RMS_EPS = 1e-6

LANES = 128
VMEM_LIMIT = 56 * 1024 * 1024


def _cparams(sem):
    return pltpu.CompilerParams(dimension_semantics=sem, vmem_limit_bytes=VMEM_LIMIT)


def _dot(a, b):
    return jnp.dot(a, b, preferred_element_type=F32)


def _dot_nt(a, b):
    return lax.dot_general(a, b, (((1,), (1,)), ((), ())), preferred_element_type=F32)


def _dot_tn(a, b):
    return lax.dot_general(a, b, (((0,), (0,)), ((), ())), preferred_element_type=F32)


def _layer_norm(v, g, b):
    mu = jnp.mean(v, axis=-1, keepdims=True)
    c = v - mu
    var = jnp.mean(c * c, axis=-1, keepdims=True)
    return c * lax.rsqrt(var + LN_EPS) * g + b


def _rms_norm(v, g):
    return v * lax.rsqrt(jnp.mean(v * v, axis=-1, keepdims=True) + RMS_EPS) * g


def _silu(v):
    return v * jax.nn.sigmoid(v)


def _linear_kernel(x_ref, w_ref, o_ref):
    o_ref[...] = _dot(x_ref[...].astype(BF16), w_ref[...]).astype(o_ref.dtype)


def _linear(x, w, *, tm, tn, out_dtype, name):
    m, k = x.shape
    n = w.shape[1]
    return pl.pallas_call(
        _linear_kernel,
        out_shape=jax.ShapeDtypeStruct((m, n), out_dtype),
        grid=(n // tn, m // tm),
        in_specs=[pl.BlockSpec((tm, k), lambda j, i: (i, 0)),
                  pl.BlockSpec((k, tn), lambda j, i: (0, j))],
        out_specs=pl.BlockSpec((tm, tn), lambda j, i: (i, j)),
        compiler_params=_cparams(("arbitrary", "arbitrary")),
        name=name,
    )(x, w)


def _linear_res_ln_kernel(x_ref, w_ref, res_ref, g_ref, b_ref, o_ref):
    y = _dot(x_ref[...].astype(BF16), w_ref[...])
    o_ref[...] = _layer_norm(ALPHA * res_ref[...] + y, g_ref[...], b_ref[...])


def _linear_res_ln(x, w, res, g, b, *, tm, name):
    m, k = x.shape
    d = w.shape[1]
    return pl.pallas_call(
        _linear_res_ln_kernel,
        out_shape=jax.ShapeDtypeStruct((m, d), F32),
        grid=(m // tm,),
        in_specs=[pl.BlockSpec((tm, k), lambda i: (i, 0)),
                  pl.BlockSpec((k, d), lambda i: (0, 0)),
                  pl.BlockSpec((tm, d), lambda i: (i, 0)),
                  pl.BlockSpec((1, d), lambda i: (0, 0)),
                  pl.BlockSpec((1, d), lambda i: (0, 0))],
        out_specs=pl.BlockSpec((tm, d), lambda i: (i, 0)),
        compiler_params=_cparams(("arbitrary",)),
        name=name,
    )(x, w, res, g, b)


def _gla_scan_matrix(c):
    levels = int(math.log2(c))
    mats = []
    for lv in range(levels):
        m = 1 << lv
        e = np.zeros((c, c), np.float32)
        for r in range(c):
            mid = (r // (2 * m)) * (2 * m) + m - 1
            if (r // m) % 2 == 1:
                e[r, mid + 1:r + 1] = 1.0
            else:
                e[r, r + 1:mid + 1] = 1.0
        mats.append(e)
    mats.append(np.tril(np.ones((c, c), np.float32)))
    mats.append(np.triu(np.ones((c, c), np.float32), 1))
    e = np.concatenate(mats, axis=0)
    e2 = np.concatenate([e, e], axis=1)
    rows = -(-e2.shape[0] // 16) * 16
    cols = max(LANES, e2.shape[1])
    out = np.zeros((rows, cols), np.float32)
    out[:e2.shape[0], :e2.shape[1]] = e2
    return out


def _gla_kernel(x_ref, s0_ref, win_ref, lb_ref, gn_ref, wout_ref, lng_ref, lnb_ref, e2_ref,
                y_ref, sout_ref, st_ref, q_s, k_s, v_s, g_s, o_s, *, c, valid, rows, seq_per_chunk, unroll):
    levels = int(math.log2(c))
    t_step = pl.program_id(1)

    if not seq_per_chunk:
        @pl.when(t_step == 0)
        def _():
            for h in range(H_A):
                st_ref[h] = s0_ref[0, h].T

    x = x_ref[...].reshape(rows, D_MODEL)
    xb = x.astype(BF16)
    lb = lb_ref[...]
    z = _dot(xb, win_ref[:, D_MODEL:2 * D_MODEL])
    f = lb + (1.0 - lb) * jax.nn.sigmoid(z)
    g = jnp.log(f)
    kk = 1.0 - f
    if valid < c:
        ok = (lax.broadcasted_iota(jnp.int32, (rows, D_MODEL), 0) & (c - 1)) < valid
        g = jnp.where(ok, g, 0.0)
        kk = jnp.where(ok, kk, 0.0)
    g_s[...] = g
    k_s[...] = kk
    q_s[...] = _silu(_dot(xb, win_ref[:, 0:D_MODEL])) * (DK_A ** -0.5)
    v_s[...] = _dot(xb, win_ref[:, 2 * D_MODEL:3 * D_MODEL])

    ri = lax.broadcasted_iota(jnp.int32, (c, c), 0)
    ci = lax.broadcasted_iota(jnp.int32, (c, c), 1)
    eye = ri == ci
    keep = [((ri >> (lv + 1)) == (ci >> (lv + 1))) & (((ri >> lv) & 1) == 1) & (((ci >> lv) & 1) == 0)
            for lv in range(levels)]

    def chunk(ic, carry):
        r0 = pl.multiple_of(ic * c, c)
        gc = g_s[pl.ds(r0, c), :]
        ghi = gc.astype(BF16).astype(F32)
        parts = [ghi, gc - ghi]
        pad = e2_ref.shape[1] - 2 * c
        if pad:
            parts.append(jnp.zeros((pad, D_MODEL), F32))
        pw = jnp.exp(_dot(e2_ref[...], jnp.concatenate(parts, axis=0).astype(BF16)))
        qc = q_s[pl.ds(r0, c), :]
        kc = k_s[pl.ds(r0, c), :]
        vc = v_s[pl.ds(r0, c), :]
        pending = []
        for h in range(H_A):
            sl = slice(h * DK_A, (h + 1) * DK_A)
            qh, kh, vh = qc[:, sl], kc[:, sl], vc[:, sl].astype(BF16)
            dots = [_dot_nt(qh.astype(BF16), kh.astype(BF16))]
            for lv in range(levels):
                p = pw[lv * c:(lv + 1) * c, sl]
                dots.append(_dot_nt((qh * p).astype(BF16), (kh * p).astype(BF16)))
            p_b = pw[levels * c:(levels + 1) * c, sl]
            p_s = pw[(levels + 1) * c:(levels + 2) * c, sl]
            st = s0_ref[ic, h].T if seq_per_chunk else st_ref[h]
            o_inter = _dot_nt((qh * p_b).astype(BF16), st.astype(BF16))
            st = st * p_b[c - 1:c, :] + _dot_tn(vh, (kh * p_s).astype(BF16))
            if seq_per_chunk:
                sout_ref[ic, h] = st.T
            else:
                st_ref[h] = st
            pending.append((sl, vh, dots, o_inter))
        for sl, vh, dots, o_inter in pending:
            a = jnp.where(eye, dots[0], 0.0)
            for lv in range(levels):
                a = a + jnp.where(keep[lv], dots[lv + 1], 0.0)
            o_s[pl.ds(r0, c), sl] = o_inter + _dot(a.astype(BF16), vh)
        return carry

    lax.fori_loop(0, rows // c, chunk, 0, unroll=unroll)

    gate = _dot(xb, win_ref[:, 3 * D_MODEL:4 * D_MODEL])
    zz = (_rms_norm(o_s[...], gn_ref[...]) * _silu(gate)).astype(BF16)
    y = _dot(zz, wout_ref[...])
    y_ref[...] = _layer_norm(ALPHA * x + y, lng_ref[...], lnb_ref[...]).reshape(y_ref.shape)

    if not seq_per_chunk:
        @pl.when(t_step == pl.num_programs(1) - 1)
        def _():
            for h in range(H_A):
                sout_ref[0, h] = st_ref[h].T


def _hgrn_layer(x, s0, s0_off, w_in, lb, g_norm, w_out, ln_g, ln_b, *, c, valid, rows, seq_per_chunk, unroll, name):
    nb, t, d = x.shape
    e2 = jnp.asarray(_gla_scan_matrix(c), BF16)
    vec = lambda: pl.BlockSpec((1, d), lambda b, i: (0, 0))
    nbb = rows // c if seq_per_chunk else 1
    tb = c if seq_per_chunk else rows
    assert (t == c) if seq_per_chunk else (t % rows == 0)
    assert nb % nbb == 0 and s0_off % nbb == 0
    kern = functools.partial(_gla_kernel, c=c, valid=valid, rows=rows, seq_per_chunk=seq_per_chunk, unroll=unroll)
    off = s0_off // nbb
    return pl.pallas_call(
        kern,
        out_shape=(jax.ShapeDtypeStruct((nb, t, d), F32),
                   jax.ShapeDtypeStruct((nb, H_A, DK_A, DK_A), F32)),
        grid=(nb // nbb, t // tb),
        in_specs=[pl.BlockSpec((nbb, tb, d), lambda b, i: (b, i, 0)),
                  pl.BlockSpec((nbb, H_A, DK_A, DK_A), lambda b, i: (b + off, 0, 0, 0)),
                  pl.BlockSpec((d, 4 * d), lambda b, i: (0, 0)),
                  vec(), vec(),
                  pl.BlockSpec((d, d), lambda b, i: (0, 0)),
                  vec(), vec(),
                  pl.BlockSpec(e2.shape, lambda b, i: (0, 0))],
        out_specs=(pl.BlockSpec((nbb, tb, d), lambda b, i: (b, i, 0)),
                   pl.BlockSpec((nbb, H_A, DK_A, DK_A), lambda b, i: (b, 0, 0, 0))),
        scratch_shapes=[pltpu.VMEM((H_A, DK_A, DK_A), F32)] + [pltpu.VMEM((rows, d), F32)] * 5,
        compiler_params=_cparams(("arbitrary", "arbitrary")),
        name=name,
    )(x, s0, w_in, lb, g_norm, w_out, ln_g, ln_b, e2)


def _mla_proj_kernel(*refs, with_kv):
    if with_kv:
        (x_ref, win_ref, gq_ref, gkv_ref, wuq_ref, wukv_ref, cos_ref, sin_ref,
         qn_ref, qr_ref, ckv_ref, kr_ref, krb_ref, kn_ref, v_ref) = refs
    else:
        (x_ref, win_ref, gq_ref, gkv_ref, wuq_ref, cos_ref, sin_ref,
         qn_ref, qr_ref, ckv_ref, kr_ref, krb_ref) = refs
    hd = H_B * DN_B
    xb = x_ref[...].astype(BF16)
    u = _dot(xb, win_ref[...])
    cos = cos_ref[...]
    sin = sin_ref[...]
    kr = u[:, 2 * Q_LORA:2 * Q_LORA + LANES] * cos + u[:, 2 * Q_LORA + LANES:2 * Q_LORA + 2 * LANES] * sin
    kr_ref[...] = kr[:, :DR_B]
    krb_ref[...] = kr.astype(BF16)
    ckv = _rms_norm(u[:, Q_LORA:2 * Q_LORA], gkv_ref[...])
    ckv_ref[...] = ckv
    cq = _rms_norm(u[:, :Q_LORA], gq_ref[...]).astype(BF16)
    q = _dot(cq, wuq_ref[...])
    cos_h = jnp.concatenate([cos] * H_B, axis=-1)
    sin_h = jnp.concatenate([sin] * H_B, axis=-1)
    qn = q[:, :hd]
    qr = q[:, hd:2 * hd] * cos_h + q[:, 2 * hd:3 * hd] * sin_h
    if with_kv:
        qn = qn * (SM_SCALE * LOG2_E)
        qr = qr * (SM_SCALE * LOG2_E)
        kv = _dot(ckv.astype(BF16), wukv_ref[...])
        kn_ref[...] = kv[:, :hd].astype(BF16)
        v_ref[...] = kv[:, hd:].T.astype(BF16)
    qn_ref[...] = qn.astype(BF16)
    qr_ref[...] = qr.astype(BF16)


def _mla_proj(x, w_in, g_q, g_kv, w_uq, w_ukv, cos, sin, *, tm, with_kv, name, seq=None):
    m, d = x.shape
    hd = H_B * DN_B
    row = lambda n: pl.BlockSpec((tm, n), lambda i: (i, 0))
    full = lambda a: pl.BlockSpec(a.shape, lambda i: (0, 0))
    ins = [x, w_in, g_q, g_kv, w_uq] + ([w_ukv] if with_kv else []) + [cos, sin]
    in_specs = [row(d), full(w_in), full(g_q), full(g_kv), full(w_uq)] + ([full(w_ukv)] if with_kv else []) \
        + [row(LANES), row(LANES)]
    out_shape = [jax.ShapeDtypeStruct((m, hd), BF16), jax.ShapeDtypeStruct((m, hd), BF16),
                 jax.ShapeDtypeStruct((m, KV_LORA), F32), jax.ShapeDtypeStruct((m, DR_B), F32),
                 jax.ShapeDtypeStruct((m, LANES), BF16)]
    out_specs = [row(hd), row(hd), row(KV_LORA), row(DR_B), row(LANES)]
    if with_kv:
        nt = seq // tm
        out_shape += [jax.ShapeDtypeStruct((m, hd), BF16), jax.ShapeDtypeStruct((m // seq, nt, hd, tm), BF16)]
        out_specs += [row(hd), pl.BlockSpec((None, None, hd, tm), lambda i: (i // nt, i % nt, 0, 0))]
    return pl.pallas_call(
        functools.partial(_mla_proj_kernel, with_kv=with_kv),
        out_shape=tuple(out_shape),
        grid=(m // tm,),
        in_specs=in_specs,
        out_specs=tuple(out_specs),
        compiler_params=_cparams(("arbitrary",)),
        name=name,
    )(*ins)


def _flash_kernel(qn_ref, qr_ref, kn_ref, kr_ref, vt_ref, o_ref, acc_ref, m_ref, l_ref, sa_ref, sb_ref, *, tq):
    i = pl.program_id(2)
    q = jnp.concatenate([qn_ref[...], qr_ref[...]], axis=-1)
    acc_ref[...] = jnp.zeros_like(acc_ref)
    m_ref[...] = jnp.full_like(m_ref, -jnp.inf)
    l_ref[...] = jnp.zeros_like(l_ref)

    def scores(j):
        r0 = pl.multiple_of(j * tq, tq)
        k = jnp.concatenate([kn_ref[pl.ds(r0, tq), :], kr_ref[pl.ds(r0, tq), :]], axis=-1)
        return _dot_nt(k, q)

    def consume(j, s_ref, masked):
        vt = vt_ref[j]
        half = tq // 2
        for c0 in (0, half):
            cols = slice(c0, c0 + half)
            s = s_ref[:, cols]
            if masked:
                ki = lax.broadcasted_iota(jnp.int32, (tq, half), 0)
                qi = lax.broadcasted_iota(jnp.int32, (tq, half), 1) + c0
                s = jnp.where(ki <= qi, s, MASK_VALUE)
            m = m_ref[:, cols]
            m_new = jnp.maximum(m, jnp.max(s, axis=0, keepdims=True))
            a = jnp.exp2(m - m_new)
            p = jnp.exp2(s - m_new)
            l_ref[:, cols] = a * l_ref[:, cols] + jnp.sum(p, axis=0, keepdims=True)
            m_ref[:, cols] = m_new
            acc_ref[:, cols] = a * acc_ref[:, cols] + _dot(vt, p.astype(BF16))

    sa_ref[...] = scores(0)

    def pair(t, carry):
        sb_ref[...] = scores(2 * t + 1)
        consume(2 * t, sa_ref, False)
        sa_ref[...] = scores(2 * t + 2)
        consume(2 * t + 1, sb_ref, False)
        return carry

    lax.fori_loop(0, i // 2, pair, 0)

    @pl.when(i % 2 == 0)
    def _():
        consume(i, sa_ref, True)

    @pl.when(i % 2 == 1)
    def _():
        sb_ref[...] = scores(i)
        consume(i - 1, sa_ref, False)
        consume(i, sb_ref, True)

    o_ref[...] = (acc_ref[...] / l_ref[...]).T.astype(o_ref.dtype)


def _flash_attention(qn, qr, kn, krb, vt, *, batch, seq, tq, name):
    nq = seq // tq
    assert vt.shape[1] == nq and vt.shape[3] == tq
    qspec = pl.BlockSpec((tq, DN_B), lambda b, h, i: (b * nq + i, h))
    return pl.pallas_call(
        functools.partial(_flash_kernel, tq=tq),
        out_shape=jax.ShapeDtypeStruct((batch * seq, H_B * DV_B), BF16),
        grid=(batch, H_B, nq),
        in_specs=[qspec, qspec,
                  pl.BlockSpec((seq, DN_B), lambda b, h, i: (b, h)),
                  pl.BlockSpec((seq, LANES), lambda b, h, i: (b, 0)),
                  pl.BlockSpec((None, nq, DV_B, tq), lambda b, h, i: (b, 0, h, 0))],
        out_specs=pl.BlockSpec((tq, DV_B), lambda b, h, i: (b * nq + i, h)),
        scratch_shapes=[pltpu.VMEM((DV_B, tq), F32), pltpu.VMEM((1, tq), F32), pltpu.VMEM((1, tq), F32),
                        pltpu.VMEM((tq, tq), F32), pltpu.VMEM((tq, tq), F32)],
        compiler_params=_cparams(("arbitrary", "arbitrary", "arbitrary")),
        name=name,
    )(qn, qr, kn, krb, vt)


def _absorb_q_kernel(q_ref, w_ref, o_ref):
    o_ref[...] = _dot_nt(q_ref[...], w_ref[...]).astype(o_ref.dtype)


def _absorb_q(qn, w_ukv, *, name):
    m = qn.shape[0]
    return pl.pallas_call(
        _absorb_q_kernel,
        out_shape=jax.ShapeDtypeStruct((H_B, m, KV_LORA), BF16),
        grid=(H_B,),
        in_specs=[pl.BlockSpec((m, DN_B), lambda h: (0, h)),
                  pl.BlockSpec((KV_LORA, DN_B), lambda h: (0, h))],
        out_specs=pl.BlockSpec((None, m, KV_LORA), lambda h: (h, 0, 0)),
        compiler_params=_cparams(("arbitrary",)),
        name=name,
    )(qn, w_ukv)


def _absorb_o_kernel(o_ref, w_ref, y_ref):
    y_ref[...] = _dot(o_ref[...].astype(BF16), w_ref[...]).astype(y_ref.dtype)


def _absorb_o(o_lat, w_ukv, *, name):
    m = o_lat.shape[1]
    return pl.pallas_call(
        _absorb_o_kernel,
        out_shape=jax.ShapeDtypeStruct((m, H_B * DV_B), BF16),
        grid=(H_B,),
        in_specs=[pl.BlockSpec((None, m, KV_LORA), lambda h: (h, 0, 0)),
                  pl.BlockSpec((KV_LORA, DV_B), lambda h: (0, H_B + h))],
        out_specs=pl.BlockSpec((m, DV_B), lambda h: (0, h)),
        compiler_params=_cparams(("arbitrary",)),
        name=name,
    )(o_lat, w_ukv)


def _mla_decode_kernel(pt_ref, ql_ref, qr_ref, cn_ref, kn_ref, ckv_hbm, kr_hbm, o_ref,
                       ckbuf, krbuf, ckb, krb, sem_c, sem_k, *, layer, n_pages, per_group, t_new, scale):
    b = pl.program_id(0)
    slot = b & 1

    def page_copies(seq, sl, j):
        pg = pt_ref[seq * n_pages + j]
        return (pltpu.make_async_copy(ckv_hbm.at[pg, layer], ckbuf.at[sl, j], sem_c.at[sl]),
                pltpu.make_async_copy(kr_hbm.at[pg, layer], krbuf.at[sl, j], sem_k.at[sl]))

    def start_fetch(seq, sl):
        def body(j, carry):
            for cp in page_copies(seq, sl, j):
                cp.start()
            return carry
        lax.fori_loop(0, n_pages, body, 0, unroll=per_group)

    @pl.when(b == 0)
    def _():
        start_fetch(0, 0)

    @pl.when(b + 1 < pl.num_programs(0))
    def _():
        start_fetch(b + 1, 1 - slot)

    def wait_body(j, carry):
        for cp in page_copies(b, slot, j):
            cp.wait()
        return carry
    lax.fori_loop(0, n_pages, wait_body, 0, unroll=per_group)

    ql = ql_ref[0]
    qr = qr_ref[0]

    for gi in range(n_pages // per_group):
        rows = slice(gi * per_group * PAGE_SIZE, (gi + 1) * per_group * PAGE_SIZE)
        ckb[rows, :] = ckbuf[slot, gi * per_group:(gi + 1) * per_group].reshape(
            per_group * PAGE_SIZE, KV_LORA).astype(BF16)
        krb[:, rows] = jnp.concatenate([krbuf[slot, gi * per_group + j].astype(BF16) for j in range(per_group)], axis=1)

    s_past = (_dot_nt(ql, ckb[...]) + _dot(qr, krb[...])) * scale
    cn = cn_ref[0]
    kn = kn_ref[0]
    s_new = (_dot_nt(ql, cn) + _dot_nt(qr, kn)) * scale
    n_rows, n_keys = s_new.shape
    t_row = lax.broadcasted_iota(jnp.int32, (n_rows, n_keys), 0) >> int(math.log2(H_B))
    key = lax.broadcasted_iota(jnp.int32, (n_rows, n_keys), 1)
    s_new = jnp.where((key <= t_row) & (key < t_new), s_new, MASK_VALUE)

    m = jnp.maximum(jnp.max(s_past, axis=-1, keepdims=True), jnp.max(s_new, axis=-1, keepdims=True))
    p_past = jnp.exp(s_past - m)
    p_new = jnp.exp(s_new - m)
    l = jnp.sum(p_past, axis=-1, keepdims=True) + jnp.sum(p_new, axis=-1, keepdims=True)
    acc = _dot(p_past.astype(BF16), ckb[...]) + _dot(p_new.astype(BF16), cn)
    o_ref[0] = (acc / l).astype(o_ref.dtype)


def _mla_decode(page_table, q_lat, q_rope, ckv_new, kr_new, cache_ckv, cache_krope_t, *, layer, t_new, name):
    nb, n_rows, _ = q_lat.shape
    n_pages = page_table.shape[1]
    pt = page_table.reshape(-1)
    scale = (DN_B + DR_B) ** -0.5
    per_group = math.gcd(n_pages, 8)
    per_b = lambda shape: pl.BlockSpec((1,) + shape, lambda b, pt_ref: (b, 0, 0))
    grid_spec = pltpu.PrefetchScalarGridSpec(
        num_scalar_prefetch=1,
        grid=(nb,),
        in_specs=[per_b((n_rows, KV_LORA)), per_b((n_rows, DR_B)),
                  per_b(ckv_new.shape[1:]), per_b(kr_new.shape[1:]),
                  pl.BlockSpec(memory_space=pl.ANY), pl.BlockSpec(memory_space=pl.ANY)],
        out_specs=per_b((n_rows, KV_LORA)),
        scratch_shapes=[pltpu.VMEM((2, n_pages, PAGE_SIZE, KV_LORA), F32),
                        pltpu.VMEM((2, n_pages, DR_B, PAGE_SIZE), F32),
                        pltpu.VMEM((n_pages * PAGE_SIZE, KV_LORA), BF16),
                        pltpu.VMEM((DR_B, n_pages * PAGE_SIZE), BF16),
                        pltpu.SemaphoreType.DMA((2,)), pltpu.SemaphoreType.DMA((2,))],
    )
    return pl.pallas_call(
        functools.partial(_mla_decode_kernel, layer=layer, n_pages=n_pages, per_group=per_group,
                          t_new=t_new, scale=scale),
        out_shape=jax.ShapeDtypeStruct((nb, n_rows, KV_LORA), F32),
        grid_spec=grid_spec,
        compiler_params=_cparams(("arbitrary",)),
        name=name,
    )(pt, q_lat, q_rope, ckv_new, kr_new, cache_ckv, cache_krope_t)


def _softmax_pv(s, v):
    m = jnp.max(s, axis=-1, keepdims=True)
    e = jnp.exp(s - m)
    return _dot(e.astype(BF16), v) / jnp.sum(e, axis=-1, keepdims=True)


def _memattn_p_kernel(x_ref, k_ref, v_ref, wq_ref, wo_ref, g_ref, b_ref, o_ref):
    x = x_ref[...]
    q = _dot(x.astype(BF16), wq_ref[...]).astype(BF16)
    heads = [slice(h * DH_M, (h + 1) * DH_M) for h in range(H_M)]
    scores = [_dot_nt(q[:, sl], k_ref[:, sl].astype(BF16)) * (DH_M ** -0.5) for sl in heads]
    outs = [_softmax_pv(s, v_ref[:, sl].astype(BF16)) for s, sl in zip(scores, heads)]
    o = jnp.concatenate(outs, axis=-1).astype(BF16)
    o_ref[...] = _layer_norm(ALPHA * x + _dot(o, wo_ref[...]), g_ref[...], b_ref[...])


def _memattn_prompt(x, mk, mv, w_q, w_o, layer, g, b, *, batch, tm, name):
    m, d = x.shape
    nt = m // batch // tm
    wspec = pl.BlockSpec((None, d, d), lambda bb, i: (layer, 0, 0))
    vec = pl.BlockSpec((1, d), lambda bb, i: (0, 0))
    kvspec = pl.BlockSpec((N_MEM, d), lambda bb, i: (bb, 0))
    return pl.pallas_call(
        _memattn_p_kernel,
        out_shape=jax.ShapeDtypeStruct((m, d), F32),
        grid=(batch, nt),
        in_specs=[pl.BlockSpec((tm, d), lambda bb, i: (bb * nt + i, 0)), kvspec, kvspec, wspec, wspec, vec, vec],
        out_specs=pl.BlockSpec((tm, d), lambda bb, i: (bb * nt + i, 0)),
        compiler_params=_cparams(("arbitrary", "arbitrary")),
        name=name,
    )(x, mk, mv, w_q, w_o, g, b)


def _mem_cache_rows(c):
    lead = c.shape[:3]
    return c.reshape(lead + (H_M, DH_M // LANES, LANES)).transpose(0, 1, 2, 4, 3, 5).reshape(-1, LANES)


def _memattn_s_kernel(x_ref, k_ref, v_ref, wq_ref, wo_ref, g_ref, b_ref, o_ref, *, nb, t_new):
    x = x_ref[...]
    rows = x.shape[0]
    q = _dot(x.astype(BF16), wq_ref[...]).astype(BF16)
    row_b = lax.broadcasted_iota(jnp.int32, (rows, DH_M), 0) >> int(math.log2(t_new))
    n_dt = DH_M // LANES
    per_m = H_M * n_dt
    per_b = N_MEM * per_m

    def head(ref, bb, h):
        parts = [ref[pl.ds(bb * per_b + dt * H_M + h, N_MEM, stride=per_m), :] for dt in range(n_dt)]
        return jnp.concatenate(parts, axis=-1).astype(BF16)

    pairs = [(h, bb) for h in range(H_M) for bb in range(nb)]
    scores = [_dot_nt(q[:, h * DH_M:(h + 1) * DH_M], head(k_ref, bb, h)) * (DH_M ** -0.5) for h, bb in pairs]
    outs = [jnp.zeros((rows, DH_M), F32) for _ in range(H_M)]
    for (h, bb), s in zip(pairs, scores):
        outs[h] = jnp.where(row_b == bb, _softmax_pv(s, head(v_ref, bb, h)), outs[h])
    o = jnp.concatenate(outs, axis=-1).astype(BF16)
    o_ref[...] = _layer_norm(ALPHA * x + _dot(o, wo_ref[...]), g_ref[...], b_ref[...])


def _memattn_sample(x, ck, cv, w_q, w_o, layer, g, b, *, nb, t_new, name):
    m, d = x.shape
    n_steps = m // (nb * t_new)
    wspec = pl.BlockSpec((None, d, d), lambda i: (layer, 0, 0))
    vec = pl.BlockSpec((1, d), lambda i: (0, 0))
    kvspec = pl.BlockSpec((nb * N_MEM * d // LANES, LANES), lambda i: (layer * n_steps + i, 0))
    return pl.pallas_call(
        functools.partial(_memattn_s_kernel, nb=nb, t_new=t_new),
        out_shape=jax.ShapeDtypeStruct((m, d), F32),
        grid=(n_steps,),
        in_specs=[pl.BlockSpec((nb * t_new, d), lambda i: (i, 0)), kvspec, kvspec, wspec, wspec, vec, vec],
        out_specs=pl.BlockSpec((nb * t_new, d), lambda i: (i, 0)),
        compiler_params=_cparams(("arbitrary",)),
        name=name,
    )(x, ck, cv, w_q, w_o, g, b)


def _moe_kernel(x_ref, wr_ref, br_ref, tri_ref, w1_ref, w3_ref, w2_ref, g_ref, b_ref, o_ref,
                pt_ref, xs_ref, gs_ref, acc_ref, blk_ref, *, scatter_terms):
    gi = pl.program_id(1)
    neg = -jnp.inf
    tm, ns = pt_ref.shape

    @pl.when(gi == 0)
    def _():
        x = x_ref[...]
        x_hi = x.astype(BF16)
        x_lo = (x - x_hi.astype(F32)).astype(BF16)
        hi2 = _dot(x_hi, wr_ref[...])
        lg = hi2[:, :LANES] + hi2[:, LANES:] + _dot(x_lo, wr_ref[:, :LANES]) + br_ref[...]
        lane_i = lax.broadcasted_iota(jnp.int32, lg.shape, 1)
        lane = lane_i.astype(F32)
        is_g = lane_i < N_GROUPS
        lgm = jnp.where(is_g, lg, neg)
        mg = jnp.max(lgm, axis=-1, keepdims=True)
        g_star = jnp.min(jnp.where(lgm == mg, lane, float(LANES)), axis=-1, keepdims=True)
        p_group = 1.0 / jnp.sum(jnp.where(is_g, jnp.exp(lg - mg), 0.0), axis=-1, keepdims=True)
        e_id = lane_i - N_GROUPS
        grp = (e_id >> int(math.log2(E_PER_GROUP))).astype(F32)
        in_grp = (e_id >= 0) & (e_id < N_EXPERTS) & (grp == g_star)
        le = jnp.where(in_grp, lg, neg)
        v1 = jnp.max(le, axis=-1, keepdims=True)
        i1 = jnp.min(jnp.where(le == v1, lane, float(LANES)), axis=-1, keepdims=True)
        le2 = jnp.where(lane == i1, neg, le)
        v2 = jnp.max(le2, axis=-1, keepdims=True)
        i2 = jnp.min(jnp.where(le2 == v2, lane, float(LANES)), axis=-1, keepdims=True)
        e2 = jnp.exp(v2 - v1)
        den = 1.0 + e2
        gates = jnp.where(lane == i1, p_group / den, jnp.where(lane == i2, p_group * e2 / den, 0.0))

        onehot = jnp.where(is_g & (lane == g_star), 1.0, 0.0)
        counts = [jnp.sum(jnp.where(lane_i == g, onehot, 0.0)).astype(jnp.int32) for g in range(N_GROUPS)]
        n_blk = [(c + (MOE_BLOCK - 1)) // MOE_BLOCK for c in counts]
        first_blk = [sum(n_blk[:g], jnp.int32(0)) for g in range(N_GROUPS)]
        for g in range(N_GROUPS):
            blk_ref[g] = first_blk[g]
            blk_ref[N_GROUPS + g] = n_blk[g]
        seg_start = jnp.zeros((1, LANES), F32)
        lane_row = lane_i[0:1, :]
        for g in range(N_GROUPS):
            seg_start = jnp.where(lane_row == g, (first_blk[g] * MOE_BLOCK).astype(F32), seg_start)
        earlier = _dot(tri_ref[...], onehot.astype(BF16))
        sel = jnp.where(onehot > 0.0, earlier + seg_start, 0.0)
        rank_col = jnp.sum(sel, axis=-1, keepdims=True)
        rank_row = lax.dot_general(jnp.ones((8, LANES), F32), sel, (((1,), (1,)), ((), ())),
                                   preferred_element_type=F32, precision=lax.Precision.HIGHEST)[0:1, :]
        col = lax.broadcasted_iota(jnp.int32, (tm, ns), 1)
        pt_ref[...] = jnp.where(col == rank_col.astype(jnp.int32), 1.0, 0.0).astype(BF16)
        row = lax.broadcasted_iota(jnp.int32, (ns, tm), 0)
        p = jnp.where(row == rank_row.astype(jnp.int32), 1.0, 0.0).astype(BF16)
        xs_ref[...] = _dot(p, x_hi).astype(BF16)
        g_hi = gates.astype(BF16)
        g_lo = (gates - g_hi.astype(F32)).astype(BF16)
        g2t = _dot(p, jnp.concatenate([g_hi, g_lo], axis=-1))
        gs_ref[...] = g2t[:, :LANES] + g2t[:, LANES:]
        acc_ref[...] = jnp.zeros_like(acc_ref)

    first = blk_ref[gi]
    lane_b = lax.broadcasted_iota(jnp.int32, (MOE_BLOCK, LANES), 1)

    def block(bi, carry):
        r0 = pl.multiple_of((first + bi) * MOE_BLOCK, MOE_BLOCK)
        xb = xs_ref[pl.ds(r0, MOE_BLOCK), :]
        gates_b = gs_ref[pl.ds(r0, MOE_BLOCK), :]
        ups = [(_dot(xb, w1_ref[j]), _dot(xb, w3_ref[j])) for j in range(E_PER_GROUP)]
        acc = jnp.zeros((MOE_BLOCK, D_MODEL), F32)
        for j, (h1, h3) in enumerate(ups):
            ge = jnp.sum(jnp.where(lane_b == N_GROUPS + gi * E_PER_GROUP + j, gates_b, 0.0), axis=-1, keepdims=True)
            acc = acc + _dot((_silu(h1) * h3 * ge).astype(BF16), w2_ref[j])
        acc_ref[pl.ds(r0, MOE_BLOCK), :] = acc
        return carry

    lax.fori_loop(0, blk_ref[N_GROUPS + gi], block, 0)

    @pl.when(gi == pl.num_programs(1) - 1)
    def _():
        acc = acc_ref[...]
        a_hi = acc.astype(BF16)
        y = _dot(pt_ref[...], a_hi)
        if scatter_terms == 2:
            y = y + _dot(pt_ref[...], (acc - a_hi.astype(F32)).astype(BF16))
        o_ref[...] = _layer_norm(ALPHA * x_ref[...] + y, g_ref[...], b_ref[...])


def _moe(x, w_r, b_r, w1, w3, w2, layer, g, b, *, tm, scatter_terms, name):
    m, d = x.shape
    ns = tm + N_GROUPS * MOE_BLOCK
    tri = jnp.asarray(np.tril(np.ones((tm, tm), np.float32), -1), BF16)
    vec = pl.BlockSpec((1, d), lambda i, e: (0, 0))
    return pl.pallas_call(
        functools.partial(_moe_kernel, scatter_terms=scatter_terms),
        out_shape=jax.ShapeDtypeStruct((m, d), F32),
        grid=(m // tm, N_GROUPS),
        in_specs=[pl.BlockSpec((tm, d), lambda i, e: (i, 0)),
                  pl.BlockSpec((None, d, 2 * LANES), lambda i, e: (layer, 0, 0)),
                  pl.BlockSpec((None, 1, LANES), lambda i, e: (layer, 0, 0)),
                  pl.BlockSpec((tm, tm), lambda i, e: (0, 0)),
                  pl.BlockSpec((None, E_PER_GROUP, d, D_FF_E), lambda i, e: (layer, e, 0, 0)),
                  pl.BlockSpec((None, E_PER_GROUP, d, D_FF_E), lambda i, e: (layer, e, 0, 0)),
                  pl.BlockSpec((None, E_PER_GROUP, D_FF_E, d), lambda i, e: (layer, e, 0, 0)),
                  vec, vec],
        out_specs=pl.BlockSpec((tm, d), lambda i, e: (i, 0)),
        scratch_shapes=[pltpu.VMEM((tm, ns), BF16), pltpu.VMEM((ns, d), BF16), pltpu.VMEM((ns, LANES), F32),
                        pltpu.VMEM((ns, d), F32), pltpu.SMEM((2 * N_GROUPS,), jnp.int32)],
        compiler_params=_cparams(("arbitrary", "arbitrary")),
        name=name,
    )(x, w_r, b_r, tri, w1, w3, w2, g, b)


def _pool_kernel(u_ref, halo_ref, x_ref, wg_ref, ls_ref, g_ref, b_ref, o_ref, buf_ref, *, tm, from_zero, use_halo):
    t_step = pl.program_id(1)
    u = u_ref[...]
    buf_ref[POOL_HALO:POOL_HALO + tm, :] = u
    if use_halo:
        @pl.when(t_step > 0)
        def _():
            buf_ref[0:POOL_HALO, :] = halo_ref[...]

        @pl.when(t_step == 0)
        def _():
            buf_ref[0:POOL_HALO, :] = jnp.zeros((POOL_HALO, D_MODEL), F32)
    else:
        buf_ref[0:POOL_HALO, :] = jnp.zeros((POOL_HALO, D_MODEL), F32)
    outs = []
    for gi, w in enumerate(POOL_WINDOWS):
        sl = slice(gi * POOL_GROUP_DIM, (gi + 1) * POOL_GROUP_DIM)
        us = u[:, sl]
        acc = us
        for dd in range(1, w):
            acc = acc + buf_ref[POOL_HALO - dd:POOL_HALO - dd + tm, sl]
        if from_zero:
            pos = t_step * tm + lax.broadcasted_iota(jnp.int32, (tm, POOL_GROUP_DIM), 0)
            cnt = jnp.minimum(w, pos + 1).astype(F32)
        else:
            cnt = float(w)
        p = (acc / cnt - us).astype(BF16)
        outs.append(_dot(p, wg_ref[gi]))
    y = jnp.concatenate(outs, axis=-1) * ls_ref[...]
    o_ref[...] = _layer_norm(ALPHA * x_ref[...] + y, g_ref[...], b_ref[...])


def _pool_layer(u, x, w_grp, ls, g, b, *, batch, tm, from_zero, use_halo, name):
    m, d = u.shape
    nt = m // batch // tm
    hb = tm // POOL_HALO
    vec = pl.BlockSpec((1, d), lambda bb, i: (0, 0))
    row = pl.BlockSpec((tm, d), lambda bb, i: (bb * nt + i, 0))
    return pl.pallas_call(
        functools.partial(_pool_kernel, tm=tm, from_zero=from_zero, use_halo=use_halo),
        out_shape=jax.ShapeDtypeStruct((m, d), F32),
        grid=(batch, nt),
        in_specs=[row,
                  pl.BlockSpec((POOL_HALO, d), lambda bb, i: (jnp.maximum((bb * nt + i) * hb - 1, 0), 0)),
                  row,
                  pl.BlockSpec(w_grp.shape, lambda bb, i: (0, 0, 0)),
                  vec, vec, vec],
        out_specs=row,
        scratch_shapes=[pltpu.VMEM((POOL_HALO + tm, d), F32)],
        compiler_params=_cparams(("arbitrary", "arbitrary")),
        name=name,
    )(u, u, x, w_grp, ls, g, b)


def _rot_cols(w):
    half = DR_B // 2
    return jnp.concatenate([-w[..., half:], w[..., :half]], axis=-1)


def _pad_lanes(w):
    return jnp.pad(w, [(0, 0)] * (w.ndim - 1) + [(0, LANES - w.shape[-1])])


def _rope_tables(pos):
    inv = ROPE_THETA ** (-jnp.arange(0, DR_B, 2, dtype=F32) / DR_B)
    ang = pos.astype(F32)[:, None] * inv[None, :]
    cos, sin = jnp.cos(ang), jnp.sin(ang)
    return (_pad_lanes(jnp.concatenate([cos, cos], axis=-1)), _pad_lanes(jnp.concatenate([sin, sin], axis=-1)))


def _mla_weights(w_in, w_uq, w_ukv):
    kr = w_in[:, Q_LORA + KV_LORA:]
    w_in_x = jnp.concatenate([w_in[:, :Q_LORA + KV_LORA], _pad_lanes(kr), _pad_lanes(_rot_cols(kr))], axis=-1)
    wq = w_uq.reshape(Q_LORA, H_B, DN_B + DR_B)
    wq_r = wq[..., DN_B:]
    w_uq_x = jnp.concatenate([wq[..., :DN_B].reshape(Q_LORA, -1),
                              _pad_lanes(wq_r).reshape(Q_LORA, -1),
                              _pad_lanes(_rot_cols(wq_r)).reshape(Q_LORA, -1)], axis=-1)
    wkv = w_ukv.reshape(KV_LORA, H_B, DN_B + DV_B)
    w_ukv_x = jnp.concatenate([wkv[..., :DN_B].reshape(KV_LORA, -1), wkv[..., DN_B:].reshape(KV_LORA, -1)], axis=-1)
    return w_in_x.astype(BF16), w_uq_x.astype(BF16), w_ukv_x.astype(BF16)


def kernel(x_prompt, x_sample, mem_prompt, state_hgrn, cache_ckv, cache_krope, cache_pool, cache_mem_k,
           cache_mem_v, page_table, ln_g, ln_b, a_w_in, a_lb_logits, a_g_norm, a_w_out, b_w_in, b_q_norm,
           b_kv_norm, b_w_uq, b_w_ukv, b_w_out, c_w_in, c_w_grp, c_scale, m_w_q, m_w_k, m_w_v, m_w_o,
           e_w_rg, e_b_rg, e_w_re, e_b_re, e_w1, e_w3, e_w2):
    bsz, seq, d = x_prompt.shape
    dbs, t_new, _ = x_sample.shape
    mp, ms = bsz * seq, dbs * t_new
    t_pad = 8

    a_w_in_b, a_w_out_b = a_w_in.astype(BF16), a_w_out.astype(BF16)
    b_w_out_b = b_w_out.astype(BF16)
    c_w_in_b, c_w_grp_b = c_w_in.astype(BF16), c_w_grp.astype(BF16)
    m_w_q_b, m_w_k_b, m_w_v_b, m_w_o_b = (w.astype(BF16) for w in (m_w_q, m_w_k, m_w_v, m_w_o))
    e_w1_b, e_w3_b, e_w2_b = e_w1.astype(BF16), e_w3.astype(BF16), e_w2.astype(BF16)
    w_router = _pad_lanes(jnp.concatenate([e_w_rg, e_w_re], axis=-1))
    w_router_hi = w_router.astype(BF16)
    w_router = jnp.concatenate([w_router_hi, (w_router - w_router_hi.astype(F32)).astype(BF16)], axis=-1)
    b_router = _pad_lanes(jnp.concatenate([e_b_rg, e_b_re], axis=-1))[:, None, :]

    lb_sm = jax.nn.softmax(a_lb_logits.astype(F32), axis=0)
    lb_all = jnp.cumsum(lb_sm, axis=0) - lb_sm[0:1]

    cmk = _mem_cache_rows(cache_mem_k)
    cmv = _mem_cache_rows(cache_mem_v)
    n_state_layers = state_hgrn.shape[0]
    state_flat = state_hgrn.reshape((n_state_layers * dbs,) + state_hgrn.shape[2:])
    zero_state = jnp.zeros((bsz,) + state_hgrn.shape[2:], F32)

    xp = x_prompt.reshape(mp, d)
    xs = x_sample.reshape(ms, d)
    vec = lambda a: a.reshape(1, -1)

    hgrn_p, hgrn_s, ckv_p, kr_p, ckv_s, kr_s, pool_p, pool_s, memk_p, memv_p = ([] for _ in range(10))

    for i in range(DEPTH):
        kind, li = i % N_MIXERS, i // N_MIXERS
        g0, b0 = vec(ln_g[i, 0]), vec(ln_b[i, 0])
        if kind == 0:
            common = (a_w_in_b[li], vec(lb_all[li]), vec(a_g_norm[li]), a_w_out_b[li], g0, b0)
            yp, sp = _hgrn_layer(xp.reshape(bsz, seq, d), zero_state, 0, *common,
                                 c=CHUNK_A, valid=CHUNK_A, rows=min(512, seq), seq_per_chunk=False, unroll=2,
                                 name=f"hgrn_prompt_{i}")
            xs_pad = jnp.pad(xs.reshape(dbs, t_new, d), ((0, 0), (0, t_pad - t_new), (0, 0)))
            ys, ss = _hgrn_layer(xs_pad, state_flat, li * dbs, *common,
                                 c=t_pad, valid=t_new, rows=8 * t_pad, seq_per_chunk=True, unroll=2,
                                 name=f"hgrn_sample_{i}")
            xp = yp.reshape(mp, d)
            xs = ys[:, :t_new].reshape(ms, d)
            hgrn_p.append(sp)
            hgrn_s.append(ss)
        elif kind == 1:
            w_in_x, w_uq_x, w_ukv_x = _mla_weights(b_w_in[li], b_w_uq[li], b_w_ukv[li])
            gq, gkv = vec(b_q_norm[li]), vec(b_kv_norm[li])
            cos_p, sin_p = _rope_tables(jnp.arange(seq, dtype=jnp.int32))
            cos_p, sin_p = jnp.tile(cos_p, (bsz, 1)), jnp.tile(sin_p, (bsz, 1))
            qn, qr, ckv, kr, krb, kn, vt = _mla_proj(xp, w_in_x, gq, gkv, w_uq_x, w_ukv_x, cos_p, sin_p,
                                                     tm=min(512, seq), with_kv=True, seq=seq,
                                                     name=f"mla_proj_prompt_{i}")
            o = _flash_attention(qn, qr, kn, krb, vt, batch=bsz, seq=seq, tq=min(512, seq), name=f"mla_flash_{i}")
            xp = _linear_res_ln(o, b_w_out_b[li], xp, g0, b0, tm=512, name=f"mla_out_prompt_{i}")
            ckv_p.append(ckv.reshape(bsz, seq, KV_LORA))
            kr_p.append(kr.reshape(bsz, seq, DR_B))
            past_len = page_table.shape[1] * PAGE_SIZE
            cos_s, sin_s = _rope_tables(past_len + jnp.arange(t_new, dtype=jnp.int32))
            cos_s, sin_s = jnp.tile(cos_s, (dbs, 1)), jnp.tile(sin_s, (dbs, 1))
            qn_s, qr_s, ckv_n, kr_n, _ = _mla_proj(xs, w_in_x, gq, gkv, w_uq_x, None, cos_s, sin_s,
                                                   tm=ms, with_kv=False, name=f"mla_proj_sample_{i}")
            q_lat = _absorb_q(qn_s, w_ukv_x, name=f"mla_absorb_q_{i}")
            q_lat = q_lat.reshape(H_B, dbs, t_new, KV_LORA).transpose(1, 2, 0, 3).reshape(dbs, t_new * H_B, KV_LORA)
            q_rope = qr_s.reshape(dbs, t_new, H_B, LANES)[..., :DR_B].reshape(dbs, t_new * H_B, DR_B)
            pad_t = ((0, 0), (0, 2 * t_pad - t_new), (0, 0))
            ckv_new = jnp.pad(ckv_n.reshape(dbs, t_new, KV_LORA), pad_t).astype(BF16)
            kr_new = jnp.pad(kr_n.reshape(dbs, t_new, DR_B), pad_t).astype(BF16)
            krope_t = jnp.swapaxes(cache_krope, 2, 3)
            o_lat = _mla_decode(page_table, q_lat, q_rope, ckv_new, kr_new, cache_ckv, krope_t,
                                layer=li, t_new=t_new, name=f"mla_decode_{i}")
            o_lat = o_lat.reshape(dbs, t_new, H_B, KV_LORA).transpose(2, 0, 1, 3).reshape(H_B, ms, KV_LORA)
            o_s = _absorb_o(o_lat, w_ukv_x, name=f"mla_absorb_o_{i}")
            xs = _linear_res_ln(o_s, b_w_out_b[li], xs, g0, b0, tm=ms, name=f"mla_out_sample_{i}")
            ckv_s.append(ckv_n.reshape(dbs, t_new, KV_LORA))
            kr_s.append(kr_n.reshape(dbs, t_new, DR_B))
        else:
            ls = vec(c_scale[li])
            up = _linear(xp, c_w_in_b[li], tm=1024, tn=d, out_dtype=F32, name=f"pool_in_prompt_{i}")
            xp = _pool_layer(up, xp, c_w_grp_b[li], ls, g0, b0, batch=bsz, tm=512,
                             from_zero=True, use_halo=True, name=f"pool_prompt_{i}")
            pool_p.append(up.reshape(bsz, seq, d)[:, -POOL_BUF:])
            us_new = _linear(xs, c_w_in_b[li], tm=ms, tn=d, out_dtype=F32, name=f"pool_in_sample_{i}")
            us = jnp.concatenate([cache_pool[li], us_new.reshape(dbs, t_new, d)], axis=1)
            seg = 2 * POOL_HALO
            lead = seg - t_new - POOL_BUF - t_new
            us_ext = jnp.pad(us, ((0, 0), (lead, t_new), (0, 0))).reshape(dbs * seg, d)
            xs_ext = jnp.pad(xs.reshape(dbs, t_new, d), ((0, 0), (seg - 2 * t_new, t_new), (0, 0))).reshape(dbs * seg, d)
            ys_ext = _pool_layer(us_ext, xs_ext, c_w_grp_b[li], ls, g0, b0, batch=1, tm=min(512, dbs * seg),
                                 from_zero=False, use_halo=False, name=f"pool_sample_{i}")
            xs = ys_ext.reshape(dbs, seg, d)[:, seg - 2 * t_new:seg - t_new].reshape(ms, d)
            pool_s.append(us[:, -POOL_BUF:])

        g1, b1 = vec(ln_g[i, 1]), vec(ln_b[i, 1])
        mem_flat = mem_prompt.reshape(bsz * N_MEM, d)
        mk = _linear(mem_flat, m_w_k_b[i], tm=bsz * N_MEM, tn=d, out_dtype=F32, name=f"mem_k_{i}")
        mv = _linear(mem_flat, m_w_v_b[i], tm=bsz * N_MEM, tn=d, out_dtype=F32, name=f"mem_v_{i}")
        memk_p.append(mk.reshape(bsz, N_MEM, H_M, DH_M))
        memv_p.append(mv.reshape(bsz, N_MEM, H_M, DH_M))
        xp = _memattn_prompt(xp, mk, mv, m_w_q_b, m_w_o_b, i, g1, b1, batch=bsz, tm=512, name=f"memattn_prompt_{i}")
        xs = _memattn_sample(xs, cmk, cmv, m_w_q_b, m_w_o_b, i, g1, b1, nb=4, t_new=t_new, name=f"memattn_sample_{i}")

        g2, b2 = vec(ln_g[i, 2]), vec(ln_b[i, 2])
        xp = _moe(xp, w_router, b_router, e_w1_b, e_w3_b, e_w2_b, i, g2, b2, tm=min(1024, mp), scatter_terms=1,
                  name=f"moe_prompt_{i}")
        xs = _moe(xs, w_router, b_router, e_w1_b, e_w3_b, e_w2_b, i, g2, b2, tm=ms, scatter_terms=2,
                  name=f"moe_sample_{i}")

    return (xp.reshape(bsz, seq, d), xs.reshape(dbs, t_new, d),
            jnp.stack(hgrn_p, axis=0), jnp.stack(hgrn_s, axis=0),
            jnp.stack(ckv_p, axis=2), jnp.stack(kr_p, axis=2),
            jnp.stack(ckv_s, axis=2), jnp.stack(kr_s, axis=2),
            jnp.stack(pool_p, axis=0), jnp.stack(pool_s, axis=0),
            jnp.stack(memk_p, axis=0), jnp.stack(memv_p, axis=0))
```
